```python
import math
import jax, jax.numpy as jnp
from jax import lax
import numpy as np

D_MODEL = 2048
BATCH = 2
SEQ = 4096
DEPTH = 4
DEC_BATCH = 8
DEC_SEQ = 4
PAST_LEN = 16384
PAGE_SIZE = 128

N_HEADS = 8
HEAD_DIM = 128
ATTN_WIDTH = N_HEADS * HEAD_DIM
CONV_CH = D_MODEL // 2
CONV_WIDTH = 31
POOL_CH = D_MODEL // 2
POOL_WINDOWS = (2, 4, 8, 16)
POOL_GROUP = POOL_CH // len(POOL_WINDOWS)
POOL_MAX = max(POOL_WINDOWS)
MIX_WIDTH = ATTN_WIDTH + CONV_CH
IN_EVEN = 3 * ATTN_WIDTH + 2 * CONV_CH
IN_ODD = POOL_CH + 3 * ATTN_WIDTH
MOBA_BLOCK = 256
MOBA_TOPK = 3
MOBA_Q_CHUNK = 32
SB_Q_BLOCK = 128
N_EXPERTS = 32
TOP_K = 4
D_FF = D_MODEL
SWIGLU_LIMIT = 7.0
SWIGLU_ALPHA = 1.702
EPS = 1e-6
N_EVEN = (DEPTH + 1) // 2
N_ODD = DEPTH // 2

kernel_name = 'hybrid_moba_conformer_pool_stickbreak_moe_step'

F32 = jnp.float32


def _rmsnorm(x):
    xf = x.astype(F32)
    return (xf * lax.rsqrt(jnp.mean(xf * xf, axis=-1, keepdims=True) + EPS)).astype(x.dtype)


def _alibi_slopes(n):
    return jnp.asarray([2.0 ** (-8.0 * (h + 1) / n) for h in range(n)], F32)


def _chunk(length, c):
    return c if length % c == 0 else length


def _map_query_chunks(fn, q, q_pos, chunk):
    B, L, H, d = q.shape
    n = L // chunk
    qc = jnp.moveaxis(q.reshape(B, n, chunk, H, d), 1, 0)
    pc = q_pos.reshape(n, chunk)
    out = lax.map(lambda a: fn(a[0], a[1]), (qc, pc))
    return jnp.moveaxis(out, 0, 1).reshape(B, L, H, -1)


def _moba_attention(q, k, v, q_pos):
    B, Lk, H, d = k.shape
    nb = -(-Lk // MOBA_BLOCK)
    pad = nb * MOBA_BLOCK - Lk
    kb = jnp.pad(k, ((0, 0), (0, pad), (0, 0), (0, 0))).reshape(B, nb, MOBA_BLOCK, H, d)
    vb = jnp.pad(v, ((0, 0), (0, pad), (0, 0), (0, 0))).reshape(B, nb, MOBA_BLOCK, H, d)
    k_mean = jnp.mean(kb.astype(F32), axis=2)
    n_sel = min(MOBA_TOPK, nb)
    slopes = _alibi_slopes(H)
    scale = 1.0 / math.sqrt(d)
    blk = jnp.arange(nb, dtype=jnp.int32)
    offs = jnp.arange(MOBA_BLOCK, dtype=jnp.int32)
    bi = jnp.arange(B)[:, None, None, None]
    hi = jnp.arange(H)[None, None, :, None]

    def chunk_fn(qc, pc):
        qn = qc.shape[1]
        own = pc // MOBA_BLOCK
        s = jnp.einsum('bqhd,bnhd->bqhn', qc.astype(F32), k_mean)
        s = jnp.where(blk[None, None, None, :] < own[None, :, None, None], s, -jnp.inf)
        _, top = lax.top_k(s, n_sel)
        own_b = jnp.broadcast_to(own[None, :, None, None], (B, qn, H, 1)).astype(top.dtype)
        sel = jnp.concatenate([top, own_b], axis=-1)
        slot_ok = jnp.concatenate([top < own_b, jnp.ones_like(own_b, dtype=bool)], axis=-1)
        kg = kb[bi, sel, :, hi]
        vg = vb[bi, sel, :, hi]
        kpos = sel[..., None] * MOBA_BLOCK + offs
        qp = pc[None, :, None, None, None]
        logits = jnp.einsum('bqhd,bqhskd->bqhsk', qc, kg, preferred_element_type=F32) * scale
        logits = logits - slopes[None, None, :, None, None] * (qp - kpos).astype(F32)
        mask = slot_ok[..., None] & (kpos <= qp)
        logits = jnp.where(mask, logits, -jnp.inf)
        p = jax.nn.softmax(logits.reshape(B, qn, H, -1), axis=-1).reshape(logits.shape)
        return jnp.einsum('bqhsk,bqhskd->bqhd', p.astype(vg.dtype), vg)

    return _map_query_chunks(chunk_fn, q, q_pos, _chunk(q.shape[1], MOBA_Q_CHUNK))


def _stick_breaking_attention(q, k, v, q_pos):
    B, Lk, H, d = k.shape
    kpos = jnp.arange(Lk, dtype=jnp.int32)
    scale = 1.0 / math.sqrt(d)

    def block_fn(qc, pc):
        z = jnp.einsum('bqhd,bkhd->bhqk', qc, k, preferred_element_type=F32) * scale
        mask = kpos[None, :] < pc[:, None]
        log_keep = jnp.where(mask, jax.nn.log_sigmoid(-z), 0.0)
        after = lax.cumsum(log_keep, axis=3, reverse=True) - log_keep
        a = jnp.where(mask, jnp.exp(jax.nn.log_sigmoid(z) + after), 0.0)
        return jnp.einsum('bhqk,bkhd->bqhd', a.astype(v.dtype), v)

    return _map_query_chunks(block_fn, q, q_pos, _chunk(q.shape[1], SB_Q_BLOCK))


def _conformer_conv(u, prefix, w, b, ln_g, ln_b):
    full = jnp.concatenate([prefix.astype(u.dtype), u], axis=1)
    y = lax.conv_general_dilated(full, w[:, None, :].astype(u.dtype), (1,), 'VALID',
                                 dimension_numbers=('NWC', 'WIO', 'NWC'),
                                 feature_group_count=u.shape[-1]) + b
    yf = y.astype(F32)
    mu = jnp.mean(yf, axis=-1, keepdims=True)
    var = jnp.mean(jnp.square(yf - mu), axis=-1, keepdims=True)
    yn = (yf - mu) * lax.rsqrt(var + EPS) * ln_g + ln_b
    return jax.nn.silu(yn).astype(u.dtype), full[:, -(CONV_WIDTH - 1):]


def _multiscale_pool(u, prefix, first_pos, w_grp, scale):
    full = jnp.concatenate([prefix.astype(u.dtype), u], axis=1)
    B, Lf, _ = full.shape
    L = u.shape[1]
    pos1 = first_pos + jnp.arange(Lf, dtype=jnp.int32) + 1
    outs = []
    for g, w in enumerate(POOL_WINDOWS):
        ug = full[..., g * POOL_GROUP:(g + 1) * POOL_GROUP].astype(F32)
        cs = jnp.cumsum(ug, axis=1)
        cs_prev = jnp.pad(cs, ((0, 0), (w, 0), (0, 0)))[:, :Lf]
        cnt = jnp.minimum(pos1, w).astype(F32)[None, :, None]
        outs.append(((cs - cs_prev) / cnt - ug)[:, -L:])
    dlt = jnp.stack(outs, axis=2)
    mixed = jnp.einsum('blgc,gce->blge', dlt.astype(u.dtype), w_grp).reshape(B, L, POOL_CH) * scale
    return mixed, full[:, -(POOL_MAX - 1):]


def _moe(h, i, router_w, router_b, moe_w1, moe_b1, moe_w2, moe_b2):
    B, L, D = h.shape
    t = h.reshape(B * L, D)
    logits = jnp.einsum('nd,de->ne', t, router_w[i], preferred_element_type=F32) + router_b[i].astype(F32)
    top_val, top_idx = lax.top_k(logits, TOP_K)
    probs = jax.nn.softmax(top_val, axis=-1)
    gates = jnp.sum(jax.nn.one_hot(top_idx, N_EXPERTS, dtype=F32) * probs[..., None], axis=1)
    out = jnp.zeros((B * L, D), F32)
    for e in range(N_EXPERTS):
        g = t @ moe_w1[i, e] + moe_b1[i, e]
        x_glu = jnp.minimum(g[:, :D_FF], SWIGLU_LIMIT)
        x_lin = jnp.clip(g[:, D_FF:], -SWIGLU_LIMIT, SWIGLU_LIMIT)
        act = x_glu * jax.nn.sigmoid(SWIGLU_ALPHA * x_glu) * (x_lin + 1.0)
        out = out + gates[:, e:e + 1] * (act @ moe_w2[i, e] + moe_b2[i, e])
    return out.astype(h.dtype).reshape(B, L, D)


def _trunk(x, c, start, get_past, conv_prefix, pool_prefix,
           ada_w, ada_b, w_in_even, w_in_odd, w_out, conv_w, conv_b, conv_ln_g, conv_ln_b,
           pool_w, pool_scale, router_w, router_b, moe_w1, moe_b1, moe_w2, moe_b2, final_g):
    B, L, _ = x.shape
    q_pos = start + jnp.arange(L, dtype=jnp.int32)
    pool_first = start - pool_prefix.shape[2]
    new_k, new_v, new_conv, new_pool = [], [], [], []
    for i in range(DEPTH):
        j = i // 2
        mod = jnp.einsum('bd,de->be', jax.nn.silu(c), ada_w[i]) + ada_b[i]
        sh1, sc1, g1, sh2, sc2, g2 = jnp.split(mod[:, None, :], 6, axis=-1)
        h = _rmsnorm(x) * (1.0 + sc1) + sh1
        past_k, past_v = get_past(i)
        if i % 2 == 0:
            p = h @ w_in_even[j]
            q, k, v = [p[..., n * ATTN_WIDTH:(n + 1) * ATTN_WIDTH].reshape(B, L, N_HEADS, HEAD_DIM)
                       for n in range(3)]
            ga = p[..., 3 * ATTN_WIDTH:3 * ATTN_WIDTH + CONV_CH]
            gb = p[..., 3 * ATTN_WIDTH + CONV_CH:]
            u = ga * jax.nn.sigmoid(gb)
            o_att = _moba_attention(q, jnp.concatenate([past_k, k], axis=1),
                                    jnp.concatenate([past_v, v], axis=1), q_pos)
            o_tok, st = _conformer_conv(u, conv_prefix[j], conv_w[j], conv_b[j], conv_ln_g[j], conv_ln_b[j])
            new_conv.append(st)
        else:
            p = h @ w_in_odd[j]
            u = p[..., :POOL_CH]
            q, k, v = [p[..., POOL_CH + n * ATTN_WIDTH:POOL_CH + (n + 1) * ATTN_WIDTH].reshape(B, L, N_HEADS, HEAD_DIM)
                       for n in range(3)]
            o_tok, st = _multiscale_pool(u, pool_prefix[j], pool_first, pool_w[j], pool_scale[j])
            o_att = _stick_breaking_attention(q, jnp.concatenate([past_k, k], axis=1),
                                              jnp.concatenate([past_v, v], axis=1), q_pos)
            new_pool.append(st)
        mix = jnp.concatenate([o_att.reshape(B, L, ATTN_WIDTH).astype(x.dtype), o_tok], axis=-1)
        x = x + g1 * (mix @ w_out[i])
        h2 = _rmsnorm(x) * (1.0 + sc2) + sh2
        x = x + g2 * _moe(h2, i, router_w, router_b, moe_w1, moe_b1, moe_w2, moe_b2)
        new_k.append(k)
        new_v.append(v)
    y = _rmsnorm(x) * final_g
    return y, jnp.stack(new_k), jnp.stack(new_v), jnp.stack(new_conv), jnp.stack(new_pool)


def setup_inputs(seed: int = 0) -> dict:
    key = jax.random.key(seed)
    ks = jax.random.split(key, 32)
    n_pages = PAST_LEN // PAGE_SIZE
    n_pool = (DEC_BATCH * n_pages * 5) // 4

    def nrm(k, shape, s=1.0):
        return s * jax.random.normal(k, shape, F32)

    page_table = jax.random.permutation(ks[8], n_pool)[:DEC_BATCH * n_pages]
    page_table = page_table.reshape(DEC_BATCH, n_pages).astype(jnp.int32)
    return {
        'x_prompt': nrm(ks[0], (BATCH, SEQ, D_MODEL)),
        'x_sample': nrm(ks[1], (DEC_BATCH, DEC_SEQ, D_MODEL)),
        'c_prompt': nrm(ks[2], (BATCH, D_MODEL)),
        'c_sample': nrm(ks[3], (DEC_BATCH, D_MODEL)),
        'cache_k': nrm(ks[4], (DEPTH, n_pool, PAGE_SIZE, N_HEADS, HEAD_DIM)),
        'cache_v': nrm(ks[5], (DEPTH, n_pool, PAGE_SIZE, N_HEADS, HEAD_DIM)),
        'state_conv': nrm(ks[6], (N_EVEN, DEC_BATCH, CONV_WIDTH - 1, CONV_CH), 0.5),
        'state_pool': nrm(ks[7], (N_ODD, DEC_BATCH, POOL_MAX - 1, POOL_CH)),
        'page_table': page_table,
        'ada_w': nrm(ks[9], (DEPTH, D_MODEL, 6 * D_MODEL), 0.5 * D_MODEL ** -0.5),
        'ada_b': nrm(ks[10], (DEPTH, 6 * D_MODEL), 0.02),
        'w_in_even': nrm(ks[11], (N_EVEN, D_MODEL, IN_EVEN), D_MODEL ** -0.5),
        'w_in_odd': nrm(ks[12], (N_ODD, D_MODEL, IN_ODD), D_MODEL ** -0.5),
        'w_out': nrm(ks[13], (DEPTH, MIX_WIDTH, D_MODEL), MIX_WIDTH ** -0.5),
        'conv_w': nrm(ks[14], (N_EVEN, CONV_WIDTH, CONV_CH), CONV_WIDTH ** -0.5),
        'conv_b': nrm(ks[15], (N_EVEN, CONV_CH), 0.02),
        'conv_ln_g': 1.0 + nrm(ks[16], (N_EVEN, CONV_CH), 0.02),
        'conv_ln_b': nrm(ks[17], (N_EVEN, CONV_CH), 0.02),
        'pool_w': nrm(ks[18], (N_ODD, len(POOL_WINDOWS), POOL_GROUP, POOL_GROUP), POOL_GROUP ** -0.5),
        'pool_scale': 1.0 + nrm(ks[19], (N_ODD, POOL_CH), 0.1),
        'router_w': nrm(ks[20], (DEPTH, D_MODEL, N_EXPERTS), D_MODEL ** -0.5),
        'router_b': nrm(ks[21], (DEPTH, N_EXPERTS), 0.01),
        'moe_w1': nrm(ks[22], (DEPTH, N_EXPERTS, D_MODEL, 2 * D_FF), D_MODEL ** -0.5),
        'moe_b1': nrm(ks[23], (DEPTH, N_EXPERTS, 2 * D_FF), 0.01),
        'moe_w2': nrm(ks[24], (DEPTH, N_EXPERTS, D_FF, D_MODEL), D_FF ** -0.5),
        'moe_b2': nrm(ks[25], (DEPTH, N_EXPERTS, D_MODEL), 0.01),
        'final_g': 1.0 + nrm(ks[26], (D_MODEL,), 0.02),
    }


def reference(x_prompt, x_sample, c_prompt, c_sample, cache_k, cache_v, state_conv, state_pool, page_table,
              ada_w, ada_b, w_in_even, w_in_odd, w_out, conv_w, conv_b, conv_ln_g, conv_ln_b,
              pool_w, pool_scale, router_w, router_b, moe_w1, moe_b1, moe_w2, moe_b2, final_g):
    weights = (ada_w, ada_b, w_in_even, w_in_odd, w_out, conv_w, conv_b, conv_ln_g, conv_ln_b,
               pool_w, pool_scale, router_w, router_b, moe_w1, moe_b1, moe_w2, moe_b2, final_g)
    Bp = x_prompt.shape[0]
    Bs = x_sample.shape[0]
    past_len = page_table.shape[1] * cache_k.shape[2]

    empty_kv = jnp.zeros((Bp, 0, N_HEADS, HEAD_DIM), x_prompt.dtype)
    conv_zero = jnp.zeros((N_EVEN, Bp, CONV_WIDTH - 1, CONV_CH), x_prompt.dtype)
    pool_empty = jnp.zeros((N_ODD, Bp, 0, POOL_CH), x_prompt.dtype)
    y_prompt, new_k_prompt, new_v_prompt, new_conv_prompt, new_pool_prompt = _trunk(
        x_prompt, c_prompt, 0, lambda i: (empty_kv, empty_kv), conv_zero, pool_empty, *weights)

    def get_past(i):
        pk = cache_k[i, page_table].reshape(Bs, past_len, N_HEADS, HEAD_DIM)
        pv = cache_v[i, page_table].reshape(Bs, past_len, N_HEADS, HEAD_DIM)
        return pk.astype(x_sample.dtype), pv.astype(x_sample.dtype)

    y_sample, new_k_sample, new_v_sample, new_conv_sample, new_pool_sample = _trunk(
        x_sample, c_sample, past_len, get_past, state_conv, state_pool, *weights)

    return (y_prompt, y_sample, new_k_prompt, new_v_prompt, new_conv_prompt, new_pool_prompt,
            new_k_sample, new_v_sample, new_conv_sample, new_pool_sample)
```

```python
import functools
import math

import jax
import jax.numpy as jnp
from jax import lax
from jax.experimental import pallas as pl
from jax.experimental.pallas import tpu as pltpu

F32 = jnp.float32
BF16 = jnp.bfloat16
I32 = jnp.int32

N_HEADS = 8
HEAD_DIM = 128
ATTN_WIDTH = N_HEADS * HEAD_DIM
MOBA_BLOCK = 256
MOBA_TOPK = 3
CONV_WIDTH = 31
CONV_HALO = 32
POOL_WINDOWS = (2, 4, 8, 16)
POOL_GROUP = 256
POOL_HALO = 16
N_EXPERTS = 32
TOP_K = 4
SWIGLU_LIMIT = 7.0
SWIGLU_ALPHA = 1.702
EPS = 1e-6
ATTN_SCALE = 1.0 / math.sqrt(HEAD_DIM)

VMEM_LIMIT_V7X = 56 * 1024 * 1024
MOE_TM = 256
NT_DIMS = (((1,), (1,)), ((), ()))
TN_DIMS = (((0,), (0,)), ((), ()))


def _params(n_axes):
    return pltpu.CompilerParams(dimension_semantics=("arbitrary",) * n_axes,
                                vmem_limit_bytes=VMEM_LIMIT_V7X)


def _sds(shape, dtype):
    return jax.ShapeDtypeStruct(shape, dtype)


def _adaln_kernel(c_ref, w_ref, b_ref, o_ref):
    c = c_ref[...]
    a = (c * jax.nn.sigmoid(c)).astype(BF16)
    o_ref[0] = jnp.dot(a, w_ref[0].astype(BF16), preferred_element_type=F32) + b_ref[0]


def _adaln(c_all, ada_w, ada_b):
    depth, d, n = ada_w.shape
    r = c_all.shape[0]
    tn = 1024
    return pl.pallas_call(
        _adaln_kernel, grid=(depth, n // tn),
        in_specs=[pl.BlockSpec((r, d), lambda i, j: (0, 0)),
                  pl.BlockSpec((1, d, tn), lambda i, j: (i, 0, j)),
                  pl.BlockSpec((1, 1, tn), lambda i, j: (i, 0, j))],
        out_specs=pl.BlockSpec((1, r, tn), lambda i, j: (i, 0, j)),
        out_shape=_sds((depth, r, n), F32), compiler_params=_params(2), name="adaln",
    )(c_all, ada_w, ada_b.reshape(depth, 1, n))


def _rms(x):
    return x * lax.rsqrt(jnp.mean(x * x, axis=-1, keepdims=True) + EPS)


def _norm_mod_kernel(x_ref, sc_ref, sh_ref, o_ref):
    o_ref[0] = (_rms(x_ref[0]) * (1.0 + sc_ref[0]) + sh_ref[0]).astype(o_ref.dtype)


def _row_tile(l, cap):
    return l if l <= cap else cap


def _norm_mod(x, sc, sh):
    b, l, d = x.shape
    tl = _row_tile(l, 512)
    row = pl.BlockSpec((1, tl, d), lambda i, j: (i, j, 0))
    per_b = pl.BlockSpec((1, 1, d), lambda i, j: (i, 0, 0))
    return pl.pallas_call(_norm_mod_kernel, grid=(b, l // tl), in_specs=[row, per_b, per_b], out_specs=row,
                          out_shape=_sds((b, l, d), BF16), compiler_params=_params(2), name="norm_mod")(x, sc, sh)


def _final_norm_kernel(x_ref, g_ref, o_ref):
    o_ref[0] = _rms(x_ref[0]) * g_ref[...]


def _final_norm(x, g):
    b, l, d = x.shape
    tl = _row_tile(l, 512)
    row = pl.BlockSpec((1, tl, d), lambda i, j: (i, j, 0))
    return pl.pallas_call(_final_norm_kernel, grid=(b, l // tl),
                          in_specs=[row, pl.BlockSpec((1, d), lambda i, j: (0, 0))], out_specs=row,
                          out_shape=_sds((b, l, d), F32), compiler_params=_params(2), name="final_norm")(x, g.reshape(1, d))


def _first_row_tile():
    return (pl.program_id(1) == 0) & (pl.program_id(2) == 0)


def _mm_kernel(a_ref, w_ref, o_ref, wbf_ref):
    @pl.when(_first_row_tile())
    def _():
        wbf_ref[...] = w_ref[0].astype(BF16)

    o_ref[0] = jnp.dot(a_ref[0], wbf_ref[...], preferred_element_type=F32)


def _mm(a, w, layer):
    b, l, k = a.shape
    n = w.shape[2]
    tl, tn = _row_tile(l, 512), 512
    return pl.pallas_call(
        _mm_kernel, grid=(n // tn, b, l // tl),
        in_specs=[pl.BlockSpec((1, tl, k), lambda j, bi, li: (bi, li, 0)),
                  pl.BlockSpec((1, k, tn), lambda j, bi, li: (layer, 0, j))],
        out_specs=pl.BlockSpec((1, tl, tn), lambda j, bi, li: (bi, li, j)),
        out_shape=_sds((b, l, n), F32), scratch_shapes=[pltpu.VMEM((k, tn), BF16)],
        compiler_params=_params(3), name="in_proj")(a, w)


def _mm_out_kernel(a1_ref, a2_ref, w1_ref, w2_ref, x_ref, g_ref, o_ref, w1bf_ref, w2bf_ref):
    @pl.when(_first_row_tile())
    def _():
        w1bf_ref[...] = w1_ref[0].astype(BF16)
        w2bf_ref[...] = w2_ref[0].astype(BF16)

    y = jnp.dot(a1_ref[0], w1bf_ref[...], preferred_element_type=F32)
    y = y + jnp.dot(a2_ref[0], w2bf_ref[...], preferred_element_type=F32)
    o_ref[0] = x_ref[0] + g_ref[0] * y


def _mm_out(a1, a2, w, layer, x, g):
    b, l, kh = a1.shape
    n = w.shape[2]
    tl, tn = _row_tile(l, 512), 512
    a_spec = pl.BlockSpec((1, tl, kh), lambda j, bi, li: (bi, li, 0))
    return pl.pallas_call(
        _mm_out_kernel, grid=(n // tn, b, l // tl),
        in_specs=[a_spec, a_spec,
                  pl.BlockSpec((1, kh, tn), lambda j, bi, li: (layer, 0, j)),
                  pl.BlockSpec((1, kh, tn), lambda j, bi, li: (layer, 1, j)),
                  pl.BlockSpec((1, tl, tn), lambda j, bi, li: (bi, li, j)),
                  pl.BlockSpec((1, 1, tn), lambda j, bi, li: (bi, 0, j))],
        out_specs=pl.BlockSpec((1, tl, tn), lambda j, bi, li: (bi, li, j)),
        out_shape=_sds((b, l, n), F32),
        scratch_shapes=[pltpu.VMEM((kh, tn), BF16), pltpu.VMEM((kh, tn), BF16)],
        compiler_params=_params(3), name="out_proj")(a1, a2, w, w, x, g)


def _block_rank(s, idx, n, axis):
    rank = jnp.zeros(s.shape, I32)
    for j2 in range(n):
        c = s[:, j2:j2 + 1] if axis == 1 else s[j2:j2 + 1, :]
        rank = rank + ((c > s) | ((c == s) & (j2 < idx))).astype(I32)
    return rank


def _kmean_kernel(k_ref, o_ref):
    n = pl.program_id(1)
    o_ref[0, pl.ds(n, 1), :] = jnp.mean(k_ref[0], axis=0, keepdims=True)


def _moba_prompt_kernel(q_ref, k_ref, v_ref, km_ref, sl_ref, o_ref):
    i = pl.program_id(2)
    t = MOBA_BLOCK
    nb = km_ref.shape[1]
    q = q_ref[0]
    s = lax.dot_general(q, km_ref[0], NT_DIMS, precision=lax.Precision.HIGHEST, preferred_element_type=F32)
    blk = lax.broadcasted_iota(I32, (t, nb), 1)
    s = jnp.where(blk < i, s, -jnp.inf)
    sel = ((blk < i) & (_block_rank(s, blk, nb, 1) < MOBA_TOPK)).astype(F32)
    qb = q.astype(BF16)
    slope = sl_ref[0][:, :1]
    rowi = lax.broadcasted_iota(I32, (t, t), 0)
    coli = lax.broadcasted_iota(I32, (t, t), 1)

    def attend(j, bias, carry):
        m, l, acc = carry
        kj = k_ref[0, pl.ds(pl.multiple_of(j * t, t), t), :].astype(BF16)
        vj = v_ref[0, pl.ds(pl.multiple_of(j * t, t), t), :].astype(BF16)
        lg = lax.dot_general(qb, kj, NT_DIMS, preferred_element_type=F32) * ATTN_SCALE
        lg = lg - slope * ((i - j) * t + rowi - coli).astype(F32) + bias
        m_new = jnp.maximum(m, jnp.max(lg, axis=1, keepdims=True))
        alpha = jnp.exp(m - m_new)
        p = jnp.exp(lg - m_new)
        l = alpha * l + jnp.sum(p, axis=1, keepdims=True)
        acc = alpha * acc + jnp.dot(p.astype(BF16), vj, preferred_element_type=F32)
        return m_new, l, acc

    init = (jnp.full((t, 1), -jnp.inf, F32), jnp.zeros((t, 1), F32), jnp.zeros((t, HEAD_DIM), F32))
    carry = attend(i, jnp.where(coli <= rowi, 0.0, -jnp.inf), init)

    def body(j, carry):
        picked = jnp.max(jnp.where(blk == j, sel, 0.0), axis=1, keepdims=True)
        return attend(j, jnp.where(picked > 0.5, 0.0, -jnp.inf), carry)

    _, l, acc = lax.fori_loop(0, i, body, carry)
    o_ref[0] = (acc / l).astype(o_ref.dtype)


def _alibi_slopes():
    s = jnp.asarray([2.0 ** (-8.0 * (h + 1) / N_HEADS) for h in range(N_HEADS)], F32)
    return jnp.broadcast_to(s[:, None, None], (N_HEADS, 1, HEAD_DIM))


def _moba_prompt(p):
    b, l, _ = p.shape
    t = MOBA_BLOCK
    nb = l // t
    kmean = pl.pallas_call(
        _kmean_kernel, grid=(b, nb),
        in_specs=[pl.BlockSpec((1, t, ATTN_WIDTH), lambda bi, n: (bi, n, 1))],
        out_specs=pl.BlockSpec((1, nb, ATTN_WIDTH), lambda bi, n: (bi, 0, 0)),
        out_shape=_sds((b, nb, ATTN_WIDTH), F32), compiler_params=_params(2), name="moba_kmean")(p)
    h = N_HEADS
    return pl.pallas_call(
        _moba_prompt_kernel, grid=(b, h, nb),
        in_specs=[pl.BlockSpec((1, t, HEAD_DIM), lambda bi, hi, i: (bi, i, hi)),
                  pl.BlockSpec((1, l, HEAD_DIM), lambda bi, hi, i: (bi, 0, h + hi)),
                  pl.BlockSpec((1, l, HEAD_DIM), lambda bi, hi, i: (bi, 0, 2 * h + hi)),
                  pl.BlockSpec((1, nb, HEAD_DIM), lambda bi, hi, i: (bi, 0, hi)),
                  pl.BlockSpec((1, 1, HEAD_DIM), lambda bi, hi, i: (hi, 0, 0))],
        out_specs=pl.BlockSpec((1, t, HEAD_DIM), lambda bi, hi, i: (bi, i, hi)),
        out_shape=_sds((b, l, ATTN_WIDTH), BF16), compiler_params=_params(3), name="moba_prompt",
    )(p, p, p, kmean, _alibi_slopes())


def _log_sigmoid_pair(z):
    t = jnp.log1p(jnp.exp(-jnp.abs(z)))
    return jnp.minimum(z, 0.0) - t, jnp.minimum(-z, 0.0) - t


def _split_bf16(x):
    hi = x.astype(BF16)
    return hi, (x - hi.astype(F32)).astype(BF16)


def _sb_prompt_kernel(q_ref, k_ref, v_ref, o_ref):
    i = pl.program_id(2)
    t = q_ref.shape[1]
    qb = q_ref[0].astype(BF16)
    rowi = lax.broadcasted_iota(I32, (t, t), 0)
    coli = lax.broadcasted_iota(I32, (t, t), 1)
    later = (rowi > coli).astype(BF16)

    def body(step, carry):
        c, acc = carry
        j = i - step
        kj = k_ref[0, pl.ds(pl.multiple_of(j * t, t), t), :].astype(BF16)
        vj = v_ref[0, pl.ds(pl.multiple_of(j * t, t), t), :].astype(BF16)
        z = lax.dot_general(qb, kj, NT_DIMS, preferred_element_type=F32) * ATTN_SCALE
        mask = (j * t + coli) < (i * t + rowi)
        ls_pos, ls_neg = _log_sigmoid_pair(z)
        lk = jnp.where(mask, ls_neg, 0.0)
        hi, lo = _split_bf16(lk)
        after = (jnp.dot(hi, later, preferred_element_type=F32)
                 + jnp.dot(lo, later, preferred_element_type=F32) + c)
        a = jnp.where(mask, jnp.exp(ls_pos + after), 0.0)
        acc = acc + jnp.dot(a.astype(BF16), vj, preferred_element_type=F32)
        return c + jnp.sum(lk, axis=1, keepdims=True), acc

    _, acc = lax.fori_loop(0, i + 1, body, (jnp.zeros((t, 1), F32), jnp.zeros((t, HEAD_DIM), F32)))
    o_ref[0] = acc.astype(o_ref.dtype)


def _sb_prompt(p, col0):
    b, l, _ = p.shape
    t = 256
    h = N_HEADS
    c0 = col0 // HEAD_DIM
    return pl.pallas_call(
        _sb_prompt_kernel, grid=(b, h, l // t),
        in_specs=[pl.BlockSpec((1, t, HEAD_DIM), lambda bi, hi, i: (bi, i, c0 + hi)),
                  pl.BlockSpec((1, l, HEAD_DIM), lambda bi, hi, i: (bi, 0, c0 + h + hi)),
                  pl.BlockSpec((1, l, HEAD_DIM), lambda bi, hi, i: (bi, 0, c0 + 2 * h + hi))],
        out_specs=pl.BlockSpec((1, t, HEAD_DIM), lambda bi, hi, i: (bi, i, hi)),
        out_shape=_sds((b, l, ATTN_WIDTH), BF16), compiler_params=_params(3), name="sb_prompt")(p, p, p)


def _page_rows(ref):
    pg = ref[0, 0]
    return pg.reshape(pg.shape[0] * pg.shape[1], pg.shape[2])


def _head_match(shape):
    row = lax.broadcasted_iota(I32, shape, 0)
    col = lax.broadcasted_iota(I32, shape, 1)
    return row, col, (row & (N_HEADS - 1)) == (col & (N_HEADS - 1))


def _moba_sample_part_kernel(pt_ref, q_ref, k0_ref, k1_ref, v0_ref, v1_ref, sl_ref,
                             o_ref, m_ref, l_ref, s_ref, *, past_len):
    n = pl.program_id(1)
    q32 = q_ref[0]
    nq = q32.shape[0]
    kk = jnp.concatenate([_page_rows(k0_ref), _page_rows(k1_ref)], axis=0)
    vv = jnp.concatenate([_page_rows(v0_ref), _page_rows(v1_ref)], axis=0)
    rows = kk.shape[0]
    kmean = (jnp.sum(k0_ref[0, 0], axis=0) + jnp.sum(k1_ref[0, 0], axis=0)) * (1.0 / MOBA_BLOCK)
    s8 = lax.dot_general(kmean, q32, NT_DIMS, precision=lax.Precision.HIGHEST, preferred_element_type=F32)
    _, _, match8 = _head_match(s8.shape)
    s_ref[0, pl.ds(n, 1), :] = jnp.sum(jnp.where(match8, s8, 0.0), axis=0, keepdims=True)

    zt = lax.dot_general(kk.astype(BF16), q32.astype(BF16), NT_DIMS, preferred_element_type=F32) * ATTN_SCALE
    row, col, match = _head_match((rows, nq))
    dist = (past_len + (col >> 3)) - (n * MOBA_BLOCK + (row >> 3))
    lg = jnp.where(match, zt - sl_ref[...] * dist.astype(F32), -jnp.inf)
    m = jnp.max(lg, axis=0, keepdims=True)
    pe = jnp.exp(lg - m)
    m_ref[0, pl.ds(n, 1), :] = m
    l_ref[0, pl.ds(n, 1), :] = jnp.sum(pe, axis=0, keepdims=True)
    o_ref[0, 0] = lax.dot_general(vv.astype(BF16), pe.astype(BF16), TN_DIMS, preferred_element_type=F32)


def _moba_sample_merge_kernel(q_ref, kn_ref, vn_ref, sl_ref, op_ref, m_ref, l_ref, s_ref, o_ref, w_ref):
    q32 = q_ref[0]
    nq = q32.shape[0]
    nb = m_ref.shape[1]
    m, l, s = m_ref[0], l_ref[0], s_ref[0]
    nidx = lax.broadcasted_iota(I32, (nb, nq), 0)
    sel = _block_rank(s, nidx, nb, 0) < MOBA_TOPK
    zt = lax.dot_general(kn_ref[0].astype(BF16), q32.astype(BF16), NT_DIMS, preferred_element_type=F32) * ATTN_SCALE
    row, col, match = _head_match(zt.shape)
    dist = (col >> 3) - (row >> 3)
    lg = jnp.where(match & (dist >= 0), zt - sl_ref[...] * dist.astype(F32), -jnp.inf)
    m_all = jnp.maximum(jnp.max(lg, axis=0, keepdims=True),
                        jnp.max(jnp.where(sel, m, -jnp.inf), axis=0, keepdims=True))
    pe = jnp.exp(lg - m_all)
    w = jnp.where(sel, jnp.exp(m - m_all), 0.0)
    w_ref[...] = w
    denom = jnp.sum(pe, axis=0, keepdims=True) + jnp.sum(w * l, axis=0, keepdims=True)
    acc = lax.dot_general(vn_ref[0].astype(BF16), pe.astype(BF16), TN_DIMS, preferred_element_type=F32)

    def body(n, acc):
        return acc + op_ref[0, n] * w_ref[pl.ds(n, 1), :]

    acc = lax.fori_loop(0, nb, body, acc)
    o_ref[0] = acc / denom


def _slopes_per_col(lq):
    s = jnp.asarray([2.0 ** (-8.0 * (h + 1) / N_HEADS) for h in range(N_HEADS)], F32)
    return jnp.tile(s, lq)[None, :]


def _moba_sample(q32, kn, vn, cache_k, cache_v, page_table, layer):
    b, nq, d = q32.shape
    n_pages = page_table.shape[1]
    page = cache_k.shape[2]
    per_blk = MOBA_BLOCK // page
    assert per_blk == 2 and n_pages % per_blk == 0
    nb = n_pages // per_blk
    past_len = n_pages * page
    slopes = _slopes_per_col(nq // N_HEADS)
    pg = lambda off: pl.BlockSpec((1, 1, page, N_HEADS, d), lambda bi, n, pt: (layer, pt[bi, 2 * n + off], 0, 0, 0))
    stat = pl.BlockSpec((1, nb, nq), lambda bi, n, pt: (bi, 0, 0))
    o_part, m, l, s = pl.pallas_call(
        functools.partial(_moba_sample_part_kernel, past_len=past_len),
        grid_spec=pltpu.PrefetchScalarGridSpec(
            num_scalar_prefetch=1, grid=(b, nb),
            in_specs=[pl.BlockSpec((1, nq, d), lambda bi, n, pt: (bi, 0, 0)), pg(0), pg(1), pg(0), pg(1),
                      pl.BlockSpec((1, nq), lambda bi, n, pt: (0, 0))],
            out_specs=[pl.BlockSpec((1, 1, d, nq), lambda bi, n, pt: (bi, n, 0, 0)), stat, stat, stat]),
        out_shape=[_sds((b, nb, d, nq), F32)] + [_sds((b, nb, nq), F32)] * 3,
        compiler_params=_params(2), name="moba_sample_part",
    )(page_table, q32, cache_k, cache_k, cache_v, cache_v, slopes)
    per_b = pl.BlockSpec((1, nq, d), lambda bi: (bi, 0, 0))
    stat1 = pl.BlockSpec((1, nb, nq), lambda bi: (bi, 0, 0))
    o_t = pl.pallas_call(
        _moba_sample_merge_kernel, grid=(b,),
        in_specs=[per_b, per_b, per_b, pl.BlockSpec((1, nq), lambda bi: (0, 0)),
                  pl.BlockSpec((1, nb, d, nq), lambda bi: (bi, 0, 0, 0)), stat1, stat1, stat1],
        out_specs=pl.BlockSpec((1, d, nq), lambda bi: (bi, 0, 0)),
        out_shape=_sds((b, d, nq), F32), scratch_shapes=[pltpu.VMEM((nb, nq), F32)],
        compiler_params=_params(1), name="moba_sample_merge",
    )(q32, kn, vn, slopes, o_part, m, l, s)
    return jnp.swapaxes(o_t, 1, 2)


def _sb_sample_kernel(pt_ref, q_ref, kn_ref, vn_ref, k_ref, v_ref, o_ref, c_ref, later_ref):
    step = pl.program_id(1)
    q32b = q_ref[0].astype(BF16)
    nq = q32b.shape[0]

    def keep_and_weights(kk, mask, later, carry):
        z = lax.dot_general(kk.astype(BF16), q32b, NT_DIMS, preferred_element_type=F32) * ATTN_SCALE
        ls_pos, ls_neg = _log_sigmoid_pair(z)
        lk = jnp.where(mask, ls_neg, 0.0)
        hi, lo = _split_bf16(lk)
        after = (jnp.dot(later, hi, preferred_element_type=F32)
                 + jnp.dot(later, lo, preferred_element_type=F32) + carry)
        a = jnp.where(mask, jnp.exp(ls_pos + after), 0.0)
        return a.astype(BF16), jnp.sum(lk, axis=0, keepdims=True)

    @pl.when((pl.program_id(0) == 0) & (step == 0))
    def _():
        r = lax.broadcasted_iota(I32, later_ref.shape, 0)
        c = lax.broadcasted_iota(I32, later_ref.shape, 1)
        later_ref[...] = ((c >> 3) > (r >> 3)).astype(BF16)

    @pl.when(step == 0)
    def _():
        row, col, match = _head_match((nq, nq))
        mask = match & ((row >> 3) < (col >> 3))
        a, lk_sum = keep_and_weights(kn_ref[0], mask, later_ref[0:nq, 0:nq], jnp.zeros((1, nq), F32))
        o_ref[0] = lax.dot_general(vn_ref[0].astype(BF16), a, TN_DIMS, preferred_element_type=F32)
        c_ref[...] = lk_sum

    kk = _page_rows(k_ref)
    _, _, match = _head_match((kk.shape[0], nq))
    a, lk_sum = keep_and_weights(kk, match, later_ref[...], c_ref[...])
    o_ref[0] += lax.dot_general(_page_rows(v_ref).astype(BF16), a, TN_DIMS, preferred_element_type=F32)
    c_ref[...] += lk_sum


def _sb_sample(q32, kn, vn, cache_k, cache_v, page_table, layer):
    b, nq, d = q32.shape
    n_pages = page_table.shape[1]
    page = cache_k.shape[2]
    rows = page * N_HEADS
    per_b = pl.BlockSpec((1, nq, d), lambda bi, s, pt: (bi, 0, 0))
    pg = pl.BlockSpec((1, 1, page, N_HEADS, d), lambda bi, s, pt: (layer, pt[bi, n_pages - 1 - s], 0, 0, 0))
    o_t = pl.pallas_call(
        _sb_sample_kernel,
        grid_spec=pltpu.PrefetchScalarGridSpec(
            num_scalar_prefetch=1, grid=(b, n_pages),
            in_specs=[per_b, per_b, per_b, pg, pg],
            out_specs=pl.BlockSpec((1, d, nq), lambda bi, s, pt: (bi, 0, 0)),
            scratch_shapes=[pltpu.VMEM((1, nq), F32), pltpu.VMEM((rows, rows), BF16)]),
        out_shape=_sds((b, d, nq), F32), compiler_params=_params(2), name="sb_sample",
    )(page_table, q32, kn, vn, cache_k, cache_v)
    return jnp.swapaxes(o_t, 1, 2)


def _conv_kernel(*refs, tl, use_halo):
    if use_halo:
        ga_ref, gb_ref, hga_ref, hgb_ref, pre_ref, w_ref, b_ref, g_ref, bt_ref, o_ref, u_ref, ext_ref, y_ref = refs
    else:
        ga_ref, gb_ref, pre_ref, w_ref, b_ref, g_ref, bt_ref, o_ref, u_ref, ext_ref, y_ref = refs
    i = pl.program_id(1)
    u = ga_ref[0] * jax.nn.sigmoid(gb_ref[0])
    u_ref[0] = u
    ext_ref[CONV_HALO:CONV_HALO + tl, :] = u

    @pl.when(i == 0)
    def _():
        ext_ref[0:CONV_HALO, :] = pre_ref[0]

    if use_halo:
        @pl.when(i > 0)
        def _():
            ext_ref[0:CONV_HALO, :] = hga_ref[0] * jax.nn.sigmoid(hgb_ref[0])

    lane = 128
    off = CONV_HALO - (CONV_WIDTH - 1)

    def chunk(c, carry):
        cs = pl.ds(pl.multiple_of(c * lane, lane), lane)
        acc = jnp.zeros((tl, lane), F32)
        for j in range(CONV_WIDTH):
            acc = acc + ext_ref[off + j:off + j + tl, cs] * w_ref[j:j + 1, cs]
        y_ref[:, cs] = acc
        return carry

    lax.fori_loop(0, ext_ref.shape[1] // lane, chunk, 0)
    y = y_ref[...] + b_ref[...]
    mu = jnp.mean(y, axis=-1, keepdims=True)
    var = jnp.mean(jnp.square(y - mu), axis=-1, keepdims=True)
    yn = (y - mu) * lax.rsqrt(var + EPS) * g_ref[...] + bt_ref[...]
    o_ref[0] = (yn * jax.nn.sigmoid(yn)).astype(o_ref.dtype)


def _conv_module(p, col0, prefix, w, bias, ln_g, ln_b):
    b, l, _ = p.shape
    c = w.shape[1]
    tl = _row_tile(l, 256)
    use_halo = l > tl
    ca, cb = col0 // c, col0 // c + 1
    pre = jnp.pad(prefix, ((0, 0), (CONV_HALO - prefix.shape[1], 0), (0, 0)))
    wp = jnp.pad(w, ((0, CONV_HALO - w.shape[0]), (0, 0)))
    row = lambda cc: pl.BlockSpec((1, tl, c), lambda bi, i: (bi, i, cc))
    hr = tl // CONV_HALO
    halo = lambda cc: pl.BlockSpec((1, CONV_HALO, c), lambda bi, i: (bi, jnp.maximum(i * hr - 1, 0), cc))
    vec = pl.BlockSpec((1, c), lambda bi, i: (0, 0))
    in_specs = [row(ca), row(cb)] + ([halo(ca), halo(cb)] if use_halo else []) + [
        pl.BlockSpec((1, CONV_HALO, c), lambda bi, i: (bi, 0, 0)),
        pl.BlockSpec((CONV_HALO, c), lambda bi, i: (0, 0)), vec, vec, vec]
    args = [p, p] + ([p, p] if use_halo else []) + [pre, wp, bias.reshape(1, c), ln_g.reshape(1, c), ln_b.reshape(1, c)]
    out_row = pl.BlockSpec((1, tl, c), lambda bi, i: (bi, i, 0))
    return pl.pallas_call(
        functools.partial(_conv_kernel, tl=tl, use_halo=use_halo), grid=(b, l // tl),
        in_specs=in_specs, out_specs=[out_row, out_row],
        out_shape=[_sds((b, l, c), BF16), _sds((b, l, c), F32)],
        scratch_shapes=[pltpu.VMEM((CONV_HALO + tl, c), F32), pltpu.VMEM((tl, c), F32)],
        compiler_params=_params(2), name="conv_module")(*args)


def _pool_kernel(*refs, tl, start, use_halo):
    if use_halo:
        u_ref, hu_ref, pre_ref, w_ref, sc_ref, o_ref, ext_ref = refs
    else:
        u_ref, pre_ref, w_ref, sc_ref, o_ref, ext_ref = refs
    i = pl.program_id(1)
    ext_ref[POOL_HALO:POOL_HALO + tl, :] = u_ref[0]

    @pl.when(i == 0)
    def _():
        ext_ref[0:POOL_HALO, :] = pre_ref[0]

    if use_halo:
        @pl.when(i > 0)
        def _():
            ext_ref[0:POOL_HALO, :] = hu_ref[0]

    pos1 = start + i * tl + lax.broadcasted_iota(I32, (tl, 1), 0) + 1
    for g, win in enumerate(POOL_WINDOWS):
        cs = slice(g * POOL_GROUP, (g + 1) * POOL_GROUP)
        ug = ext_ref[POOL_HALO:POOL_HALO + tl, cs]
        ws = ug
        for dlt in range(1, win):
            ws = ws + ext_ref[POOL_HALO - dlt:POOL_HALO - dlt + tl, cs]
        cnt = jnp.minimum(pos1, win).astype(F32)
        delta = ws / cnt - ug
        mixed = jnp.dot(delta.astype(BF16), w_ref[0, g].astype(BF16), preferred_element_type=F32)
        o_ref[0, :, cs] = (mixed * sc_ref[:, cs]).astype(o_ref.dtype)


def _pool_module(p, prefix, start, w_all, layer_j, scale):
    b, l, _ = p.shape
    c = scale.shape[0]
    tl = _row_tile(l, 256)
    use_halo = l > tl
    pre = jnp.pad(prefix, ((0, 0), (POOL_HALO - prefix.shape[1], 0), (0, 0)))
    row = pl.BlockSpec((1, tl, c), lambda bi, i: (bi, i, 0))
    hr = tl // POOL_HALO
    halo = pl.BlockSpec((1, POOL_HALO, c), lambda bi, i: (bi, jnp.maximum(i * hr - 1, 0), 0))
    in_specs = [row] + ([halo] if use_halo else []) + [
        pl.BlockSpec((1, POOL_HALO, c), lambda bi, i: (bi, 0, 0)),
        pl.BlockSpec((1,) + w_all.shape[1:], lambda bi, i: (layer_j, 0, 0, 0)),
        pl.BlockSpec((1, c), lambda bi, i: (0, 0))]
    args = [p] + ([p] if use_halo else []) + [pre, w_all, scale.reshape(1, c)]
    return pl.pallas_call(
        functools.partial(_pool_kernel, tl=tl, start=start, use_halo=use_halo), grid=(b, l // tl),
        in_specs=in_specs, out_specs=row, out_shape=_sds((b, l, c), BF16),
        scratch_shapes=[pltpu.VMEM((POOL_HALO + tl, c), F32)],
        compiler_params=_params(2), name="pool_module")(*args)


def _router_kernel(x_ref, sc_ref, sh_ref, rw_ref, rb_ref, h_ref, idx_ref, prob_ref):
    h = _rms(x_ref[0]) * (1.0 + sc_ref[0]) + sh_ref[0]
    h_ref[0] = h
    lg = jnp.dot(h, rw_ref[0], precision=lax.Precision.HIGHEST, preferred_element_type=F32) + rb_ref[0]
    lane = lax.broadcasted_iota(I32, lg.shape, 1)
    idxs, vals = [], []
    for _ in range(TOP_K):
        m = jnp.max(lg, axis=1, keepdims=True)
        ix = jnp.min(jnp.where(lg == m, lane, N_EXPERTS), axis=1, keepdims=True)
        idxs.append(ix)
        vals.append(m)
        lg = jnp.where(lane == ix, -jnp.inf, lg)
    es = [jnp.exp(v - vals[0]) for v in vals]
    tot = es[0] + es[1] + es[2] + es[3]
    idx_ref[0] = jnp.concatenate(idxs, axis=1)
    prob_ref[0] = jnp.concatenate([e / tot for e in es], axis=1)


def _router(x, sc, sh, router_w, router_b, layer):
    b, l, d = x.shape
    e = router_w.shape[2]
    tl = _row_tile(l, 256)
    row = pl.BlockSpec((1, tl, d), lambda i, j: (i, j, 0))
    per_b = pl.BlockSpec((1, 1, d), lambda i, j: (i, 0, 0))
    top = pl.BlockSpec((1, tl, TOP_K), lambda i, j: (i, j, 0))
    return pl.pallas_call(
        _router_kernel, grid=(b, l // tl),
        in_specs=[row, per_b, per_b, pl.BlockSpec((1, d, e), lambda i, j: (layer, 0, 0)),
                  pl.BlockSpec((1, 1, e), lambda i, j: (layer, 0, 0))],
        out_specs=[row, top, top],
        out_shape=[_sds((b, l, d), F32), _sds((b, l, TOP_K), I32), _sds((b, l, TOP_K), F32)],
        compiler_params=_params(2), name="router",
    )(x, sc, sh, router_w, router_b.reshape(router_b.shape[0], 1, e))


def _moe_plan(idx_flat, tm, n_tiles):
    e = N_EXPERTS
    oh = (idx_flat[:, None] == jnp.arange(e, dtype=I32)[None, :]).astype(I32)
    csum = jnp.cumsum(oh, axis=0)
    counts = csum[-1]
    tiles_e = (counts + tm - 1) // tm
    tile_end = jnp.cumsum(tiles_e)
    row_start = (tile_end - tiles_e) * tm
    pos = jnp.sum(oh * (row_start[None, :] + csum - oh), axis=1).astype(I32)
    n_valid = tile_end[-1]
    tile_row = jnp.minimum(jnp.arange(n_tiles, dtype=I32), n_valid - 1)
    tile_e = jnp.sum((tile_row[:, None] >= tile_end[None, :]).astype(I32), axis=1).astype(I32)
    return pos, tile_row, tile_e, n_valid.reshape(1).astype(I32)


def _dispatch_kernel(pos_ref, h_ref, xs_in_ref, xs_ref, sem, *, tt, base):
    del xs_in_ref
    t0 = pl.program_id(0) * tt

    def row_copy(t, r):
        return pltpu.make_async_copy(h_ref.at[pl.ds(t, 1), :], xs_ref.at[pl.ds(r, 1), :], sem)

    def issue(t, c):
        for k in range(TOP_K):
            row_copy(t, pos_ref[base + (t0 + t) * TOP_K + k]).start()
        return c

    lax.fori_loop(0, tt, issue, 0)

    def drain(t, c):
        for k in range(TOP_K):
            row_copy(0, 0).wait()
        return c

    lax.fori_loop(0, tt, drain, 0)


def _dispatch(pos, h2d, xs, base):
    t, d = h2d.shape
    tt = _row_tile(t, 256)
    return pl.pallas_call(
        functools.partial(_dispatch_kernel, tt=tt, base=base),
        grid_spec=pltpu.PrefetchScalarGridSpec(
            num_scalar_prefetch=1, grid=(t // tt,),
            in_specs=[pl.BlockSpec((tt, d), lambda i, pos: (i, 0)), pl.BlockSpec(memory_space=pl.ANY)],
            out_specs=pl.BlockSpec(memory_space=pl.ANY),
            scratch_shapes=[pltpu.SemaphoreType.DMA]),
        out_shape=_sds(xs.shape, xs.dtype), input_output_aliases={2: 0},
        compiler_params=_params(1), name="moe_dispatch")(pos, h2d, xs)


def _expert_changed(te_ref, r):
    return (r == 0) | (te_ref[r] != te_ref[jnp.maximum(r - 1, 0)])


def _moe_up_kernel(tr_ref, te_ref, nv_ref, x_ref, wg_ref, wl_ref, bg_ref, bl_ref, o_ref, wgbf_ref, wlbf_ref):
    r = pl.program_id(1)

    @pl.when(_expert_changed(te_ref, r))
    def _():
        wgbf_ref[...] = wg_ref[0, 0].astype(BF16)
        wlbf_ref[...] = wl_ref[0, 0].astype(BF16)

    @pl.when(r < nv_ref[0])
    def _():
        x = x_ref[...].astype(BF16)
        g = jnp.dot(x, wgbf_ref[...], preferred_element_type=F32) + bg_ref[0, 0]
        lin = jnp.dot(x, wlbf_ref[...], preferred_element_type=F32) + bl_ref[0, 0]
        xg = jnp.minimum(g, SWIGLU_LIMIT)
        xl = jnp.clip(lin, -SWIGLU_LIMIT, SWIGLU_LIMIT)
        o_ref[...] = (xg * jax.nn.sigmoid(SWIGLU_ALPHA * xg) * (xl + 1.0)).astype(o_ref.dtype)


def _moe_down_kernel(tr_ref, te_ref, nv_ref, a_ref, w_ref, b_ref, o_ref, wbf_ref):
    r = pl.program_id(1)

    @pl.when(_expert_changed(te_ref, r))
    def _():
        wbf_ref[...] = w_ref[0, 0].astype(BF16)

    @pl.when(r < nv_ref[0])
    def _():
        o_ref[...] = jnp.dot(a_ref[...], wbf_ref[...], preferred_element_type=F32) + b_ref[0, 0]


def _moe_experts(xs, plan, w1, b1, w2, b2, layer):
    _, tile_row, tile_e, n_valid = plan
    r_pad, d = xs.shape
    tm = MOE_TM
    n_tiles = r_pad // tm
    f = w2.shape[2]
    tn = 512
    nf = f // tn
    depth, e = b1.shape[0], b1.shape[1]
    b1r = b1.reshape(depth, e, 1, 2 * f)
    b2r = b2.reshape(depth, e, 1, d)
    act = pl.pallas_call(
        _moe_up_kernel,
        grid_spec=pltpu.PrefetchScalarGridSpec(
            num_scalar_prefetch=3, grid=(nf, n_tiles),
            in_specs=[pl.BlockSpec((tm, d), lambda n, r, tr, te, nv: (tr[r], 0)),
                      pl.BlockSpec((1, 1, d, tn), lambda n, r, tr, te, nv: (layer, te[r], 0, n)),
                      pl.BlockSpec((1, 1, d, tn), lambda n, r, tr, te, nv: (layer, te[r], 0, n + nf)),
                      pl.BlockSpec((1, 1, 1, tn), lambda n, r, tr, te, nv: (layer, te[r], 0, n)),
                      pl.BlockSpec((1, 1, 1, tn), lambda n, r, tr, te, nv: (layer, te[r], 0, n + nf))],
            out_specs=pl.BlockSpec((tm, tn), lambda n, r, tr, te, nv: (tr[r], n)),
            scratch_shapes=[pltpu.VMEM((d, tn), BF16), pltpu.VMEM((d, tn), BF16)]),
        out_shape=_sds((r_pad, f), BF16), compiler_params=_params(2), name="moe_up",
    )(tile_row, tile_e, n_valid, xs, w1, w1, b1r, b1r)
    nd = d // tn
    return pl.pallas_call(
        _moe_down_kernel,
        grid_spec=pltpu.PrefetchScalarGridSpec(
            num_scalar_prefetch=3, grid=(nd, n_tiles),
            in_specs=[pl.BlockSpec((tm, f), lambda n, r, tr, te, nv: (tr[r], 0)),
                      pl.BlockSpec((1, 1, f, tn), lambda n, r, tr, te, nv: (layer, te[r], 0, n)),
                      pl.BlockSpec((1, 1, 1, tn), lambda n, r, tr, te, nv: (layer, te[r], 0, n))],
            out_specs=pl.BlockSpec((tm, tn), lambda n, r, tr, te, nv: (tr[r], n)),
            scratch_shapes=[pltpu.VMEM((f, tn), BF16)]),
        out_shape=_sds((r_pad, d), F32), compiler_params=_params(2), name="moe_down",
    )(tile_row, tile_e, n_valid, act, w2, b2r)


def _combine_kernel(pos_ref, x_ref, g_ref, prob_ref, ys_ref, o_ref, buf_ref, sem, *, tt, base):
    t0 = (pl.program_id(0) * pl.num_programs(1) + pl.program_id(1)) * tt

    def row_copy(t, k, r):
        return pltpu.make_async_copy(ys_ref.at[pl.ds(r, 1), :], buf_ref.at[k, pl.ds(t, 1), :], sem)

    def issue(t, c):
        for k in range(TOP_K):
            row_copy(t, k, pos_ref[base + (t0 + t) * TOP_K + k]).start()
        return c

    lax.fori_loop(0, tt, issue, 0)

    def drain(t, c):
        for k in range(TOP_K):
            row_copy(0, 0, 0).wait()
        return c

    lax.fori_loop(0, tt, drain, 0)
    prob = prob_ref[0]
    acc = prob[:, 0:1] * buf_ref[0]
    for k in range(1, TOP_K):
        acc = acc + prob[:, k:k + 1] * buf_ref[k]
    o_ref[0] = x_ref[0] + g_ref[0] * acc


def _combine(pos, x, g, prob, ys, base):
    b, l, d = x.shape
    tt = _row_tile(l, 128)
    row = pl.BlockSpec((1, tt, d), lambda i, j, pos: (i, j, 0))
    return pl.pallas_call(
        functools.partial(_combine_kernel, tt=tt, base=base),
        grid_spec=pltpu.PrefetchScalarGridSpec(
            num_scalar_prefetch=1, grid=(b, l // tt),
            in_specs=[row, pl.BlockSpec((1, 1, d), lambda i, j, pos: (i, 0, 0)),
                      pl.BlockSpec((1, tt, TOP_K), lambda i, j, pos: (i, j, 0)),
                      pl.BlockSpec(memory_space=pl.ANY)],
            out_specs=row,
            scratch_shapes=[pltpu.VMEM((TOP_K, tt, d), F32), pltpu.SemaphoreType.DMA]),
        out_shape=_sds((b, l, d), F32), compiler_params=_params(2), name="moe_combine")(pos, x, g, prob, ys)


def _moe_layer(xs_groups, mods, router_w, router_b, w1, b1, w2, b2, layer):
    d = xs_groups[0].shape[-1]
    routed = [_router(x, sc, sh, router_w, router_b, layer) for x, (sc, sh, _) in zip(xs_groups, mods)]
    idx_flat = jnp.concatenate([r[1].reshape(-1) for r in routed])
    n_pairs = idx_flat.shape[0]
    tm = MOE_TM
    n_tiles = -(-n_pairs // tm) + N_EXPERTS
    plan = _moe_plan(idx_flat, tm, n_tiles)
    pos = plan[0]
    xs = jnp.zeros((n_tiles * tm, d), F32)
    bases, base = [], 0
    for h, _, _ in routed:
        bases.append(base)
        xs = _dispatch(pos, h.reshape(-1, d), xs, base)
        base += h.shape[0] * h.shape[1] * TOP_K
    ys = _moe_experts(xs, plan, w1, b1, w2, b2, layer)
    return [_combine(pos, x, g, r[2], ys, bs) for x, (_, _, g), r, bs in zip(xs_groups, mods, routed, bases)]


def _rows_token_head(p, col0):
    b, l, _ = p.shape
    return p[..., col0:col0 + ATTN_WIDTH].reshape(b, l * N_HEADS, HEAD_DIM)


def kernel(x_prompt, x_sample, c_prompt, c_sample, cache_k, cache_v, state_conv, state_pool, page_table, ada_w, ada_b, w_in_even, w_in_odd, w_out, conv_w, conv_b, conv_ln_g, conv_ln_b, pool_w, pool_scale, router_w, router_b, moe_w1, moe_b1, moe_w2, moe_b2, final_g):
    depth = ada_w.shape[0]
    d = x_prompt.shape[-1]
    bp, lp, _ = x_prompt.shape
    bs, ls, _ = x_sample.shape
    past_len = page_table.shape[1] * cache_k.shape[2]
    conv_ch = conv_w.shape[2]
    pool_ch = pool_scale.shape[1]

    n_c = bp + bs
    c_all = jnp.pad(jnp.concatenate([c_prompt, c_sample], axis=0), ((0, -n_c % 8), (0, 0)))
    mod_all = _adaln(c_all, ada_w, ada_b)

    def mods(i, lo, hi):
        m = mod_all[i, lo:hi].reshape(hi - lo, 1, 6, d)
        return [m[:, :, n, :] for n in range(6)]

    xs = [x_prompt, x_sample]
    groups = [(0, bp), (bp, n_c)]
    conv_pre = [jnp.zeros((state_conv.shape[0], bp) + state_conv.shape[2:], F32), state_conv]
    pool_pre = [jnp.zeros((state_pool.shape[0], bp) + state_pool.shape[2:], F32), state_pool]
    starts = [0, past_len]
    new_k, new_v, new_conv, new_pool = [[], []], [[], []], [[], []], [[], []]

    for i in range(depth):
        j = i // 2
        mod = [mods(i, lo, hi) for lo, hi in groups]
        for gi in range(2):
            x = xs[gi]
            b, l, _ = x.shape
            sh1, sc1, g1 = mod[gi][0], mod[gi][1], mod[gi][2]
            h = _norm_mod(x, sc1, sh1)
            if i % 2 == 0:
                p = _mm(h, w_in_even, j)
                qkv0, tok0 = 0, 3 * ATTN_WIDTH
            else:
                p = _mm(h, w_in_odd, j)
                qkv0, tok0 = pool_ch, 0
            k = p[..., qkv0 + ATTN_WIDTH:qkv0 + 2 * ATTN_WIDTH]
            v = p[..., qkv0 + 2 * ATTN_WIDTH:qkv0 + 3 * ATTN_WIDTH]
            new_k[gi].append(k.reshape(b, l, N_HEADS, HEAD_DIM))
            new_v[gi].append(v.reshape(b, l, N_HEADS, HEAD_DIM))
            if gi == 0:
                o_att = _moba_prompt(p) if i % 2 == 0 else _sb_prompt(p, qkv0)
            else:
                q32 = _rows_token_head(p, qkv0)
                kn = _rows_token_head(p, qkv0 + ATTN_WIDTH)
                vn = _rows_token_head(p, qkv0 + 2 * ATTN_WIDTH)
                attend = _moba_sample if i % 2 == 0 else _sb_sample
                o = attend(q32, kn, vn, cache_k, cache_v, page_table, i)
                o_att = o.reshape(b, l, ATTN_WIDTH).astype(BF16)
            if i % 2 == 0:
                o_tok, u = _conv_module(p, tok0, conv_pre[gi][j], conv_w[j], conv_b[j], conv_ln_g[j], conv_ln_b[j])
                full = jnp.concatenate([conv_pre[gi][j], u[:, -min(l, CONV_WIDTH - 1):]], axis=1)
                new_conv[gi].append(full[:, -(CONV_WIDTH - 1):])
            else:
                o_tok = _pool_module(p, pool_pre[gi][j], starts[gi], pool_w, j, pool_scale[j])
                n_keep = pool_pre[gi].shape[2]
                if gi == 0:
                    new_pool[gi].append(p[:, -n_keep:, :pool_ch])
                else:
                    full = jnp.concatenate([pool_pre[gi][j], p[..., :pool_ch]], axis=1)
                    new_pool[gi].append(full[:, -n_keep:])
            xs[gi] = _mm_out(o_att, o_tok, w_out, i, x, g1)
        moe_mods = [(m[4], m[3], m[5]) for m in mod]
        xs = _moe_layer(xs, moe_mods, router_w, router_b, moe_w1, moe_b1, moe_w2, moe_b2, i)

    y = [_final_norm(x, final_g) for x in xs]
    stack = lambda parts: jnp.stack(parts)
    return (y[0], y[1], stack(new_k[0]), stack(new_v[0]), stack(new_conv[0]), stack(new_pool[0]),
            stack(new_k[1]), stack(new_v[1]), stack(new_conv[1]), stack(new_pool[1]))
```

```python
import functools
import math

import jax
import jax.numpy as jnp
from jax import lax
from jax.experimental import pallas as pl
from jax.experimental.pallas import tpu as pltpu

F32 = jnp.float32
BF16 = jnp.bfloat16
I32 = jnp.int32

N_HEADS = 8
HEAD_DIM = 128
ATTN_WIDTH = N_HEADS * HEAD_DIM
MOBA_BLOCK = 256
MOBA_TOPK = 3
CONV_WIDTH = 31
CONV_HALO = 32
POOL_WINDOWS = (2, 4, 8, 16)
POOL_GROUP = 256
POOL_HALO = 16
N_EXPERTS = 32
TOP_K = 4
SWIGLU_LIMIT = 7.0
SWIGLU_ALPHA = 1.702
EPS = 1e-6
ATTN_SCALE = 1.0 / math.sqrt(HEAD_DIM)

VMEM_LIMIT_V7X = 56 * 1024 * 1024
MOE_TM = 256
MOE_TN_UP = 512
MOE_TN_DOWN = 1024
HEADS_PER_STEP = 2
QPAD = 8
PAGES_PER_STEP = 4
NT_DIMS = (((1,), (1,)), ((), ()))


def _params(n_axes):
    return pltpu.CompilerParams(dimension_semantics=("arbitrary",) * n_axes,
                                vmem_limit_bytes=VMEM_LIMIT_V7X)


def _sds(shape, dtype):
    return jax.ShapeDtypeStruct(shape, dtype)


def _adaln_kernel(c_ref, w_ref, b_ref, o_ref):
    c = c_ref[...]
    a = (c * jax.nn.sigmoid(c)).astype(BF16)
    o_ref[0] = jnp.dot(a, w_ref[0].astype(BF16), preferred_element_type=F32) + b_ref[0]


def _adaln(c_all, ada_w, ada_b):
    depth, d, n = ada_w.shape
    r = c_all.shape[0]
    tn = 1024
    return pl.pallas_call(
        _adaln_kernel, grid=(depth, n // tn),
        in_specs=[pl.BlockSpec((r, d), lambda i, j: (0, 0)),
                  pl.BlockSpec((1, d, tn), lambda i, j: (i, 0, j)),
                  pl.BlockSpec((1, 1, tn), lambda i, j: (i, 0, j))],
        out_specs=pl.BlockSpec((1, r, tn), lambda i, j: (i, 0, j)),
        out_shape=_sds((depth, r, n), F32), compiler_params=_params(2), name="adaln",
    )(c_all, ada_w, ada_b.reshape(depth, 1, n))


def _rms(x):
    return x * lax.rsqrt(jnp.mean(x * x, axis=-1, keepdims=True) + EPS)


def _norm_mod_kernel(x_ref, sc_ref, sh_ref, o_ref):
    o_ref[0] = (_rms(x_ref[0]) * (1.0 + sc_ref[0]) + sh_ref[0]).astype(o_ref.dtype)


def _row_tile(l, cap):
    return l if l <= cap else cap


def _norm_mod(x, sc, sh):
    b, l, d = x.shape
    tl = _row_tile(l, 512)
    row = pl.BlockSpec((1, tl, d), lambda i, j: (i, j, 0))
    per_b = pl.BlockSpec((1, 1, d), lambda i, j: (i, 0, 0))
    return pl.pallas_call(_norm_mod_kernel, grid=(b, l // tl), in_specs=[row, per_b, per_b], out_specs=row,
                          out_shape=_sds((b, l, d), BF16), compiler_params=_params(2), name="norm_mod")(x, sc, sh)


def _final_norm_kernel(x_ref, g_ref, o_ref):
    o_ref[0] = _rms(x_ref[0]) * g_ref[...]


def _final_norm(x, g):
    b, l, d = x.shape
    tl = _row_tile(l, 512)
    row = pl.BlockSpec((1, tl, d), lambda i, j: (i, j, 0))
    return pl.pallas_call(_final_norm_kernel, grid=(b, l // tl),
                          in_specs=[row, pl.BlockSpec((1, d), lambda i, j: (0, 0))], out_specs=row,
                          out_shape=_sds((b, l, d), F32), compiler_params=_params(2), name="final_norm")(x, g.reshape(1, d))


def _first_row_tile():
    return (pl.program_id(1) == 0) & (pl.program_id(2) == 0)


def _mm_kernel(a_ref, w_ref, o_ref, wbf_ref):
    @pl.when(_first_row_tile())
    def _():
        wbf_ref[...] = w_ref[0].astype(BF16)

    o_ref[0] = jnp.dot(a_ref[0], wbf_ref[...], preferred_element_type=F32)


def _mm(a, w, layer):
    b, l, k = a.shape
    n = w.shape[2]
    tl, tn = _row_tile(l, 512), 512
    return pl.pallas_call(
        _mm_kernel, grid=(n // tn, b, l // tl),
        in_specs=[pl.BlockSpec((1, tl, k), lambda j, bi, li: (bi, li, 0)),
                  pl.BlockSpec((1, k, tn), lambda j, bi, li: (layer, 0, j))],
        out_specs=pl.BlockSpec((1, tl, tn), lambda j, bi, li: (bi, li, j)),
        out_shape=_sds((b, l, n), F32), scratch_shapes=[pltpu.VMEM((k, tn), BF16)],
        compiler_params=_params(3), name="in_proj")(a, w)


def _mm_out_kernel(a1_ref, a2_ref, w1_ref, w2_ref, x_ref, g_ref, o_ref, w1bf_ref, w2bf_ref):
    @pl.when(_first_row_tile())
    def _():
        w1bf_ref[...] = w1_ref[0].astype(BF16)
        w2bf_ref[...] = w2_ref[0].astype(BF16)

    y = jnp.dot(a1_ref[0], w1bf_ref[...], preferred_element_type=F32)
    y = y + jnp.dot(a2_ref[0], w2bf_ref[...], preferred_element_type=F32)
    o_ref[0] = x_ref[0] + g_ref[0] * y


def _mm_out(a1, a2, w, layer, x, g):
    b, l, kh = a1.shape
    n = w.shape[2]
    tl, tn = _row_tile(l, 512), 512
    a_spec = pl.BlockSpec((1, tl, kh), lambda j, bi, li: (bi, li, 0))
    return pl.pallas_call(
        _mm_out_kernel, grid=(n // tn, b, l // tl),
        in_specs=[a_spec, a_spec,
                  pl.BlockSpec((1, kh, tn), lambda j, bi, li: (layer, 0, j)),
                  pl.BlockSpec((1, kh, tn), lambda j, bi, li: (layer, 1, j)),
                  pl.BlockSpec((1, tl, tn), lambda j, bi, li: (bi, li, j)),
                  pl.BlockSpec((1, 1, tn), lambda j, bi, li: (bi, 0, j))],
        out_specs=pl.BlockSpec((1, tl, tn), lambda j, bi, li: (bi, li, j)),
        out_shape=_sds((b, l, n), F32),
        scratch_shapes=[pltpu.VMEM((kh, tn), BF16), pltpu.VMEM((kh, tn), BF16)],
        compiler_params=_params(3), name="out_proj")(a1, a2, w, w, x, g)


def _block_rank(s, idx, n, axis):
    rank = jnp.zeros(s.shape, I32)
    for j2 in range(n):
        c = s[:, j2:j2 + 1] if axis == 1 else s[j2:j2 + 1, :]
        rank = rank + ((c > s) | ((c == s) & (j2 < idx))).astype(I32)
    return rank


def _head_cols(h):
    return slice(h * HEAD_DIM, (h + 1) * HEAD_DIM)


def _kmean_kernel(k_ref, o_ref):
    n = pl.program_id(1)
    o_ref[0, pl.ds(n, 1), :] = jnp.mean(k_ref[0], axis=0, keepdims=True)


def _moba_prompt_kernel(q_ref, k_ref, v_ref, km_ref, sl_ref, o_ref):
    i = pl.program_id(2)
    t = MOBA_BLOCK
    nb = km_ref.shape[1]
    nh = q_ref.shape[2] // HEAD_DIM
    rowi = lax.broadcasted_iota(I32, (t, t), 0)
    coli = lax.broadcasted_iota(I32, (t, t), 1)
    eye = (rowi == coli).astype(BF16)
    blk_t = lax.broadcasted_iota(I32, (nb, t), 0)
    blk = lax.broadcasted_iota(I32, (t, nb), 1)
    qbs, sels, slopes = [], [], []
    for h in range(nh):
        q = q_ref[0, :, _head_cols(h)]
        s_t = lax.dot_general(km_ref[0, :, _head_cols(h)], q, NT_DIMS,
                              precision=lax.Precision.HIGHEST, preferred_element_type=F32)
        s_t = jnp.where(blk_t < i, s_t, -jnp.inf)
        sel_t = ((blk_t < i) & (_block_rank(s_t, blk_t, nb, 0) < MOBA_TOPK)).astype(BF16)
        sels.append(lax.dot_general(eye, sel_t, NT_DIMS, preferred_element_type=F32))
        qbs.append(q.astype(BF16))
        slopes.append(sl_ref[h][:, :1])

    def attend(h, j, bias, carry):
        m, l, acc = carry
        rows = pl.ds(pl.multiple_of(j * t, t), t)
        kj = k_ref[0, rows, _head_cols(h)].astype(BF16)
        vj = v_ref[0, rows, _head_cols(h)].astype(BF16)
        lg = lax.dot_general(qbs[h], kj, NT_DIMS, preferred_element_type=F32) * ATTN_SCALE
        lg = lg - slopes[h] * ((i - j) * t + rowi - coli).astype(F32) + bias
        m_new = jnp.maximum(m, jnp.max(lg, axis=1, keepdims=True))
        alpha = jnp.exp(m - m_new)
        p = jnp.exp(lg - m_new)
        l = alpha * l + jnp.sum(p, axis=1, keepdims=True)
        acc = alpha * acc + jnp.dot(p.astype(BF16), vj, preferred_element_type=F32)
        return m_new, l, acc

    init = (jnp.full((t, 1), -jnp.inf, F32), jnp.zeros((t, 1), F32), jnp.zeros((t, HEAD_DIM), F32))
    causal = jnp.where(coli <= rowi, 0.0, -jnp.inf)
    carry = tuple(attend(h, i, causal, init) for h in range(nh))

    def body(j, carry):
        out = []
        for h in range(nh):
            picked = jnp.max(jnp.where(blk == j, sels[h], 0.0), axis=1, keepdims=True)
            out.append(attend(h, j, jnp.where(picked > 0.5, 0.0, -jnp.inf), carry[h]))
        return tuple(out)

    carry = lax.fori_loop(0, i, body, carry)
    for h in range(nh):
        _, l, acc = carry[h]
        o_ref[0, :, _head_cols(h)] = (acc / l).astype(o_ref.dtype)


def _alibi_slopes():
    return jnp.asarray([2.0 ** (-8.0 * (h + 1) / N_HEADS) for h in range(N_HEADS)], F32)


def _moba_prompt(p):
    b, l, _ = p.shape
    t = MOBA_BLOCK
    nb = l // t
    kmean = pl.pallas_call(
        _kmean_kernel, grid=(b, nb),
        in_specs=[pl.BlockSpec((1, t, ATTN_WIDTH), lambda bi, n: (bi, n, 1))],
        out_specs=pl.BlockSpec((1, nb, ATTN_WIDTH), lambda bi, n: (bi, 0, 0)),
        out_shape=_sds((b, nb, ATTN_WIDTH), F32), compiler_params=_params(2), name="moba_kmean")(p)
    nh = HEADS_PER_STEP
    w = nh * HEAD_DIM
    hg = N_HEADS // nh
    slopes = jnp.broadcast_to(_alibi_slopes()[:, None, None], (N_HEADS, 1, HEAD_DIM))
    return pl.pallas_call(
        _moba_prompt_kernel, grid=(b, hg, nb),
        in_specs=[pl.BlockSpec((1, t, w), lambda bi, hi, i: (bi, i, hi)),
                  pl.BlockSpec((1, l, w), lambda bi, hi, i: (bi, 0, hg + hi)),
                  pl.BlockSpec((1, l, w), lambda bi, hi, i: (bi, 0, 2 * hg + hi)),
                  pl.BlockSpec((1, nb, w), lambda bi, hi, i: (bi, 0, hi)),
                  pl.BlockSpec((nh, 1, HEAD_DIM), lambda bi, hi, i: (hi, 0, 0))],
        out_specs=pl.BlockSpec((1, t, w), lambda bi, hi, i: (bi, i, hi)),
        out_shape=_sds((b, l, ATTN_WIDTH), BF16), compiler_params=_params(3), name="moba_prompt",
    )(p, p, p, kmean, slopes)


def _log_sigmoid_pair(z):
    t = jnp.log(1.0 + jnp.exp(-jnp.abs(z)))
    return jnp.minimum(z, 0.0) - t, jnp.minimum(-z, 0.0) - t


def _split_bf16(x):
    hi = x.astype(BF16)
    return hi, (x - hi.astype(F32)).astype(BF16)


def _suffix_sums(lk, later):
    hi, lo = _split_bf16(lk)
    return jnp.dot(hi, later, preferred_element_type=F32) + jnp.dot(lo, later, preferred_element_type=F32)


def _sb_prompt_kernel(q_ref, k_ref, v_ref, o_ref):
    i = pl.program_id(2)
    t = q_ref.shape[1]
    nh = q_ref.shape[2] // HEAD_DIM
    qbs = [q_ref[0, :, _head_cols(h)].astype(BF16) for h in range(nh)]
    rowi = lax.broadcasted_iota(I32, (t, t), 0)
    coli = lax.broadcasted_iota(I32, (t, t), 1)
    later = (rowi > coli).astype(BF16)

    def block(j, mask, carry):
        rows = pl.ds(pl.multiple_of(j * t, t), t)
        out = []
        for h in range(nh):
            c, acc = carry[h]
            kj = k_ref[0, rows, _head_cols(h)].astype(BF16)
            vj = v_ref[0, rows, _head_cols(h)].astype(BF16)
            z = lax.dot_general(qbs[h], kj, NT_DIMS, preferred_element_type=F32) * ATTN_SCALE
            ls_pos, ls_neg = _log_sigmoid_pair(z)
            lk = ls_neg if mask is None else jnp.where(mask, ls_neg, 0.0)
            a = jnp.exp(ls_pos + _suffix_sums(lk, later) + c)
            if mask is not None:
                a = jnp.where(mask, a, 0.0)
            acc = acc + jnp.dot(a.astype(BF16), vj, preferred_element_type=F32)
            out.append((c + jnp.sum(lk, axis=1, keepdims=True), acc))
        return tuple(out)

    init = tuple((jnp.zeros((t, 1), F32), jnp.zeros((t, HEAD_DIM), F32)) for _ in range(nh))
    carry = block(i, coli < rowi, init)
    carry = lax.fori_loop(1, i + 1, lambda step, carry: block(i - step, None, carry), carry)
    for h in range(nh):
        o_ref[0, :, _head_cols(h)] = carry[h][1].astype(o_ref.dtype)


def _sb_prompt(p, col0):
    b, l, _ = p.shape
    t = 256
    nh = HEADS_PER_STEP
    w = nh * HEAD_DIM
    hg = N_HEADS // nh
    c0 = col0 // w
    return pl.pallas_call(
        _sb_prompt_kernel, grid=(b, hg, l // t),
        in_specs=[pl.BlockSpec((1, t, w), lambda bi, hi, i: (bi, i, c0 + hi)),
                  pl.BlockSpec((1, l, w), lambda bi, hi, i: (bi, 0, c0 + hg + hi)),
                  pl.BlockSpec((1, l, w), lambda bi, hi, i: (bi, 0, c0 + 2 * hg + hi))],
        out_specs=pl.BlockSpec((1, t, w), lambda bi, hi, i: (bi, i, hi)),
        out_shape=_sds((b, l, ATTN_WIDTH), BF16), compiler_params=_params(3), name="sb_prompt")(p, p, p)


def _head_rows(ref, h):
    return ref[0, 0, pl.ds(h, ref.shape[2] // N_HEADS, stride=N_HEADS), :]


def _q_rows(q_ref, h):
    return q_ref[0, h * QPAD:(h + 1) * QPAD, :]


def _qk_scores(q_ref, k_ref, with_key_sums=False):
    zs, sums = [], []
    for h in range(N_HEADS):
        kh = _head_rows(k_ref, h)
        zs.append(lax.dot_general(_q_rows(q_ref, h).astype(BF16), kh.astype(BF16), NT_DIMS,
                                  preferred_element_type=F32))
        if with_key_sums:
            sums.append(jnp.sum(kh, axis=0, keepdims=True))
    z = jnp.concatenate(zs, axis=0) * ATTN_SCALE
    return (z, sums) if with_key_sums else z


def _pv(a, v_ref):
    outs = []
    for h in range(N_HEADS):
        ah = a[h * QPAD:(h + 1) * QPAD, :].astype(BF16)
        outs.append(jnp.dot(ah, _head_rows(v_ref, h).astype(BF16), preferred_element_type=F32))
    return jnp.concatenate(outs, axis=0)


def _query_index(shape):
    return lax.broadcasted_iota(I32, shape, 0) & (QPAD - 1)


def _sb_page(q_ref, k_ref, v_ref, mask, later, c):
    z = _qk_scores(q_ref, k_ref)
    ls_pos, ls_neg = _log_sigmoid_pair(z)
    lk = ls_neg if mask is None else jnp.where(mask, ls_neg, 0.0)
    a = jnp.exp(ls_pos + _suffix_sums(lk, later) + c)
    if mask is not None:
        a = jnp.where(mask, a, 0.0)
    return _pv(a, v_ref), c + jnp.sum(lk, axis=1, keepdims=True)


def _sb_sample_kernel(pt_ref, q_ref, kn_ref, vn_ref, *refs):
    del pt_ref
    pp = PAGES_PER_STEP
    k_refs, v_refs = refs[:pp], refs[pp:2 * pp]
    o_ref, c_ref = refs[2 * pp], refs[2 * pp + 1]
    rows = q_ref.shape[1]
    page = k_refs[0].shape[2] // N_HEADS
    later = (lax.broadcasted_iota(I32, (page, page), 0) > lax.broadcasted_iota(I32, (page, page), 1)).astype(BF16)

    @pl.when(pl.program_id(1) == 0)
    def _():
        mask = lax.broadcasted_iota(I32, (rows, page), 1) < _query_index((rows, page))
        o, c = _sb_page(q_ref, kn_ref, vn_ref, mask, later, jnp.zeros((rows, 1), F32))
        o_ref[0] = o
        c_ref[...] = c

    acc, c = o_ref[0], c_ref[...]
    for k_ref, v_ref in zip(k_refs, v_refs):
        o, c = _sb_page(q_ref, k_ref, v_ref, None, later, c)
        acc = acc + o
    o_ref[0] = acc
    c_ref[...] = c


def _stack_queries(q):
    b, lq, _ = q.shape
    assert lq <= QPAD
    q = q.reshape(b, lq, N_HEADS, HEAD_DIM).transpose(0, 2, 1, 3)
    return jnp.pad(q, ((0, 0), (0, 0), (0, QPAD - lq), (0, 0))).reshape(b, N_HEADS * QPAD, HEAD_DIM)


def _unstack_queries(o, lq):
    b = o.shape[0]
    o = o.reshape(b, N_HEADS, QPAD, HEAD_DIM)[:, :, :lq]
    return o.transpose(0, 2, 1, 3).reshape(b, lq, ATTN_WIDTH)


def _new_page(x, page):
    b, lq, _ = x.shape
    return jnp.pad(x, ((0, 0), (0, page - lq), (0, 0))).reshape(b, 1, page * N_HEADS, HEAD_DIM)


def _page_view(cache):
    depth, n_pool, page, h, d = cache.shape
    return cache.reshape(depth, n_pool, page * h, d)


def _sb_sample(q, k_new, v_new, cache_k, cache_v, page_table, layer):
    b, lq, _ = q.shape
    n_pages = page_table.shape[1]
    page = cache_k.shape[2]
    pp = PAGES_PER_STEP
    assert n_pages % pp == 0
    rows = N_HEADS * QPAD
    d = HEAD_DIM
    per_b = pl.BlockSpec((1, rows, d), lambda bi, s, pt: (bi, 0, 0))
    new = pl.BlockSpec((1, 1, page * N_HEADS, d), lambda bi, s, pt: (bi, 0, 0, 0))
    pg = lambda j: pl.BlockSpec((1, 1, page * N_HEADS, d),
                                lambda bi, s, pt: (layer, pt[bi, n_pages - 1 - (s * pp + j)], 0, 0))
    ck, cv = _page_view(cache_k), _page_view(cache_v)
    o = pl.pallas_call(
        _sb_sample_kernel,
        grid_spec=pltpu.PrefetchScalarGridSpec(
            num_scalar_prefetch=1, grid=(b, n_pages // pp),
            in_specs=[per_b, new, new] + [pg(j) for j in range(pp)] * 2,
            out_specs=per_b, scratch_shapes=[pltpu.VMEM((rows, 1), F32)]),
        out_shape=_sds((b, rows, d), F32), compiler_params=_params(2), name="sb_sample",
    )(page_table, _stack_queries(q), _new_page(k_new, page), _new_page(v_new, page), *([ck] * pp), *([cv] * pp))
    return _unstack_queries(o, lq)


def _moba_sample_part_kernel(pt_ref, q_ref, sl_ref, *refs, past_len):
    del pt_ref
    pp = PAGES_PER_STEP
    k_refs, v_refs = refs[:pp], refs[pp:2 * pp]
    o_ref, st_ref = refs[2 * pp], refs[2 * pp + 1]
    rows = q_ref.shape[1]
    page = k_refs[0].shape[2] // N_HEADS
    per_blk = MOBA_BLOCK // page
    qpos = past_len + _query_index((rows, page))
    lane = lax.broadcasted_iota(I32, (rows, page), 1)
    stat_lane = lax.broadcasted_iota(I32, (rows, HEAD_DIM), 1)
    for blk in range(pp // per_blk):
        n = pl.program_id(1) * (pp // per_blk) + blk
        lgs, key_sums = [], None
        for j in range(per_blk):
            z, sums = _qk_scores(q_ref, k_refs[blk * per_blk + j], with_key_sums=True)
            kpos = n * MOBA_BLOCK + j * page + lane
            lgs.append(z - sl_ref[...] * (qpos - kpos).astype(F32))
            key_sums = sums if key_sums is None else [a + b for a, b in zip(key_sums, sums)]
        m = functools.reduce(jnp.maximum, [jnp.max(lg, axis=1, keepdims=True) for lg in lgs])
        pes = [jnp.exp(lg - m) for lg in lgs]
        l = functools.reduce(jnp.add, [jnp.sum(pe, axis=1, keepdims=True) for pe in pes])
        o = functools.reduce(jnp.add, [_pv(pe, v_refs[blk * per_blk + j]) for j, pe in enumerate(pes)])
        s = jnp.concatenate([jnp.sum(_q_rows(q_ref, h) * (key_sums[h] * (1.0 / MOBA_BLOCK)), axis=1, keepdims=True)
                             for h in range(N_HEADS)], axis=0)
        o_ref[0, blk] = o
        st_ref[0, blk] = jnp.where(stat_lane == 0, m, jnp.where(stat_lane == 1, l, jnp.where(stat_lane == 2, s, 0.0)))


def _moba_sample_merge_kernel(q_ref, sl_ref, kn_ref, vn_ref, op_ref, m_ref, l_ref, s_ref, o_ref):
    rows = q_ref.shape[1]
    m, l, s = m_ref[0], l_ref[0], s_ref[0]
    nb = m.shape[1]
    sel = _block_rank(s, lax.broadcasted_iota(I32, (rows, nb), 1), nb, 1) < MOBA_TOPK
    z = _qk_scores(q_ref, kn_ref)
    dist = _query_index(z.shape) - lax.broadcasted_iota(I32, z.shape, 1)
    lg = jnp.where(dist >= 0, z - sl_ref[...] * dist.astype(F32), -jnp.inf)
    m_all = jnp.maximum(jnp.max(lg, axis=1, keepdims=True),
                        jnp.max(jnp.where(sel, m, -jnp.inf), axis=1, keepdims=True))
    pe = jnp.exp(lg - m_all)
    w = jnp.where(sel, jnp.exp(m - m_all), 0.0)
    denom = jnp.sum(pe, axis=1, keepdims=True) + jnp.sum(w * l, axis=1, keepdims=True)
    acc = _pv(pe, vn_ref)
    for n in range(nb):
        acc = acc + w[:, n:n + 1] * op_ref[0, n]
    o_ref[0] = acc / denom


def _moba_sample(q, k_new, v_new, cache_k, cache_v, page_table, layer):
    b, lq, _ = q.shape
    n_pages = page_table.shape[1]
    page = cache_k.shape[2]
    pp = PAGES_PER_STEP
    per_blk = MOBA_BLOCK // page
    assert MOBA_BLOCK % page == 0 and pp % per_blk == 0 and n_pages % pp == 0
    nb = n_pages // per_blk
    bps = pp // per_blk
    past_len = n_pages * page
    rows = N_HEADS * QPAD
    d = HEAD_DIM
    slopes = jnp.repeat(_alibi_slopes(), QPAD)[:, None]
    qs = _stack_queries(q)
    ck, cv = _page_view(cache_k), _page_view(cache_v)
    pg = lambda j: pl.BlockSpec((1, 1, page * N_HEADS, d), lambda bi, s, pt: (layer, pt[bi, s * pp + j], 0, 0))
    part = pl.BlockSpec((1, bps, rows, d), lambda bi, s, pt: (bi, s, 0, 0))
    o_part, stats = pl.pallas_call(
        functools.partial(_moba_sample_part_kernel, past_len=past_len),
        grid_spec=pltpu.PrefetchScalarGridSpec(
            num_scalar_prefetch=1, grid=(b, n_pages // pp),
            in_specs=[pl.BlockSpec((1, rows, d), lambda bi, s, pt: (bi, 0, 0)),
                      pl.BlockSpec((rows, 1), lambda bi, s, pt: (0, 0))] + [pg(j) for j in range(pp)] * 2,
            out_specs=[part, part]),
        out_shape=[_sds((b, nb, rows, d), F32)] * 2, compiler_params=_params(2), name="moba_sample_part",
    )(page_table, qs, slopes, *([ck] * pp), *([cv] * pp))
    m, l, s = [jnp.swapaxes(stats[..., c], 1, 2) for c in range(3)]
    per_b = pl.BlockSpec((1, rows, d), lambda bi: (bi, 0, 0))
    new = pl.BlockSpec((1, 1, page * N_HEADS, d), lambda bi: (bi, 0, 0, 0))
    stat = pl.BlockSpec((1, rows, nb), lambda bi: (bi, 0, 0))
    o = pl.pallas_call(
        _moba_sample_merge_kernel, grid=(b,),
        in_specs=[per_b, pl.BlockSpec((rows, 1), lambda bi: (0, 0)), new, new,
                  pl.BlockSpec((1, nb, rows, d), lambda bi: (bi, 0, 0, 0)), stat, stat, stat],
        out_specs=per_b, out_shape=_sds((b, rows, d), F32), compiler_params=_params(1), name="moba_sample_merge",
    )(qs, slopes, _new_page(k_new, page), _new_page(v_new, page), o_part, m, l, s)
    return _unstack_queries(o, lq)


def _conv_kernel(*refs, tl, use_halo):
    if use_halo:
        ga_ref, gb_ref, hga_ref, hgb_ref, pre_ref, w_ref, b_ref, g_ref, bt_ref, o_ref, u_ref, ext_ref, y_ref = refs
    else:
        ga_ref, gb_ref, pre_ref, w_ref, b_ref, g_ref, bt_ref, o_ref, u_ref, ext_ref, y_ref = refs
    i = pl.program_id(1)
    u = ga_ref[0] * jax.nn.sigmoid(gb_ref[0])
    u_ref[0] = u
    ext_ref[CONV_HALO:CONV_HALO + tl, :] = u

    @pl.when(i == 0)
    def _():
        ext_ref[0:CONV_HALO, :] = pre_ref[0]

    if use_halo:
        @pl.when(i > 0)
        def _():
            ext_ref[0:CONV_HALO, :] = hga_ref[0] * jax.nn.sigmoid(hgb_ref[0])

    lane = 128
    off = CONV_HALO - (CONV_WIDTH - 1)

    def chunk(c, carry):
        cs = pl.ds(pl.multiple_of(c * lane, lane), lane)
        acc = jnp.zeros((tl, lane), F32)
        for j in range(CONV_WIDTH):
            acc = acc + ext_ref[off + j:off + j + tl, cs] * w_ref[j:j + 1, cs]
        y_ref[:, cs] = acc
        return carry

    lax.fori_loop(0, ext_ref.shape[1] // lane, chunk, 0)
    y = y_ref[...] + b_ref[...]
    mu = jnp.mean(y, axis=-1, keepdims=True)
    var = jnp.mean(jnp.square(y - mu), axis=-1, keepdims=True)
    yn = (y - mu) * lax.rsqrt(var + EPS) * g_ref[...] + bt_ref[...]
    o_ref[0] = (yn * jax.nn.sigmoid(yn)).astype(o_ref.dtype)


def _conv_module(p, col0, prefix, w, bias, ln_g, ln_b):
    b, l, _ = p.shape
    c = w.shape[1]
    tl = _row_tile(l, 256)
    use_halo = l > tl
    ca, cb = col0 // c, col0 // c + 1
    pre = jnp.pad(prefix, ((0, 0), (CONV_HALO - prefix.shape[1], 0), (0, 0)))
    wp = jnp.pad(w, ((0, CONV_HALO - w.shape[0]), (0, 0)))
    row = lambda cc: pl.BlockSpec((1, tl, c), lambda bi, i: (bi, i, cc))
    hr = tl // CONV_HALO
    halo = lambda cc: pl.BlockSpec((1, CONV_HALO, c), lambda bi, i: (bi, jnp.maximum(i * hr - 1, 0), cc))
    vec = pl.BlockSpec((1, c), lambda bi, i: (0, 0))
    in_specs = [row(ca), row(cb)] + ([halo(ca), halo(cb)] if use_halo else []) + [
        pl.BlockSpec((1, CONV_HALO, c), lambda bi, i: (bi, 0, 0)),
        pl.BlockSpec((CONV_HALO, c), lambda bi, i: (0, 0)), vec, vec, vec]
    args = [p, p] + ([p, p] if use_halo else []) + [pre, wp, bias.reshape(1, c), ln_g.reshape(1, c), ln_b.reshape(1, c)]
    out_row = pl.BlockSpec((1, tl, c), lambda bi, i: (bi, i, 0))
    return pl.pallas_call(
        functools.partial(_conv_kernel, tl=tl, use_halo=use_halo), grid=(b, l // tl),
        in_specs=in_specs, out_specs=[out_row, out_row],
        out_shape=[_sds((b, l, c), BF16), _sds((b, l, c), F32)],
        scratch_shapes=[pltpu.VMEM((CONV_HALO + tl, c), F32), pltpu.VMEM((tl, c), F32)],
        compiler_params=_params(2), name="conv_module")(*args)


def _pool_kernel(*refs, tl, start, use_halo):
    if use_halo:
        u_ref, hu_ref, pre_ref, w_ref, sc_ref, o_ref, ext_ref = refs
    else:
        u_ref, pre_ref, w_ref, sc_ref, o_ref, ext_ref = refs
    i = pl.program_id(1)
    ext_ref[POOL_HALO:POOL_HALO + tl, :] = u_ref[0]

    @pl.when(i == 0)
    def _():
        ext_ref[0:POOL_HALO, :] = pre_ref[0]

    if use_halo:
        @pl.when(i > 0)
        def _():
            ext_ref[0:POOL_HALO, :] = hu_ref[0]

    pos1 = start + i * tl + lax.broadcasted_iota(I32, (tl, 1), 0) + 1
    for g, win in enumerate(POOL_WINDOWS):
        cs = slice(g * POOL_GROUP, (g + 1) * POOL_GROUP)
        ug = ext_ref[POOL_HALO:POOL_HALO + tl, cs]
        ws = ug
        for dlt in range(1, win):
            ws = ws + ext_ref[POOL_HALO - dlt:POOL_HALO - dlt + tl, cs]
        cnt = jnp.minimum(pos1, win).astype(F32)
        delta = ws / cnt - ug
        mixed = jnp.dot(delta.astype(BF16), w_ref[0, g].astype(BF16), preferred_element_type=F32)
        o_ref[0, :, cs] = (mixed * sc_ref[:, cs]).astype(o_ref.dtype)


def _pool_module(p, prefix, start, w_all, layer_j, scale):
    b, l, _ = p.shape
    c = scale.shape[0]
    tl = _row_tile(l, 256)
    use_halo = l > tl
    pre = jnp.pad(prefix, ((0, 0), (POOL_HALO - prefix.shape[1], 0), (0, 0)))
    row = pl.BlockSpec((1, tl, c), lambda bi, i: (bi, i, 0))
    hr = tl // POOL_HALO
    halo = pl.BlockSpec((1, POOL_HALO, c), lambda bi, i: (bi, jnp.maximum(i * hr - 1, 0), 0))
    in_specs = [row] + ([halo] if use_halo else []) + [
        pl.BlockSpec((1, POOL_HALO, c), lambda bi, i: (bi, 0, 0)),
        pl.BlockSpec((1,) + w_all.shape[1:], lambda bi, i: (layer_j, 0, 0, 0)),
        pl.BlockSpec((1, c), lambda bi, i: (0, 0))]
    args = [p] + ([p] if use_halo else []) + [pre, w_all, scale.reshape(1, c)]
    return pl.pallas_call(
        functools.partial(_pool_kernel, tl=tl, start=start, use_halo=use_halo), grid=(b, l // tl),
        in_specs=in_specs, out_specs=row, out_shape=_sds((b, l, c), BF16),
        scratch_shapes=[pltpu.VMEM((POOL_HALO + tl, c), F32)],
        compiler_params=_params(2), name="pool_module")(*args)


def _router_kernel(x_ref, sc_ref, sh_ref, rw_ref, rb_ref, h_ref, idx_ref, prob_ref):
    h = _rms(x_ref[0]) * (1.0 + sc_ref[0]) + sh_ref[0]
    h_ref[0] = h
    lg = jnp.dot(h, rw_ref[0], precision=lax.Precision.HIGHEST, preferred_element_type=F32) + rb_ref[0]
    lane = lax.broadcasted_iota(I32, lg.shape, 1)
    idxs, vals = [], []
    for _ in range(TOP_K):
        m = jnp.max(lg, axis=1, keepdims=True)
        ix = jnp.min(jnp.where(lg == m, lane, N_EXPERTS), axis=1, keepdims=True)
        idxs.append(ix)
        vals.append(m)
        lg = jnp.where(lane == ix, -jnp.inf, lg)
    es = [jnp.exp(v - vals[0]) for v in vals]
    tot = es[0] + es[1] + es[2] + es[3]
    idx_ref[0] = jnp.concatenate(idxs, axis=1)
    prob_ref[0] = jnp.concatenate([e / tot for e in es], axis=1)


def _router(x, sc, sh, router_w, router_b, layer):
    b, l, d = x.shape
    e = router_w.shape[2]
    tl = _row_tile(l, 256)
    row = pl.BlockSpec((1, tl, d), lambda i, j: (i, j, 0))
    per_b = pl.BlockSpec((1, 1, d), lambda i, j: (i, 0, 0))
    top = pl.BlockSpec((1, tl, TOP_K), lambda i, j: (i, j, 0))
    return pl.pallas_call(
        _router_kernel, grid=(b, l // tl),
        in_specs=[row, per_b, per_b, pl.BlockSpec((1, d, e), lambda i, j: (layer, 0, 0)),
                  pl.BlockSpec((1, 1, e), lambda i, j: (layer, 0, 0))],
        out_specs=[row, top, top],
        out_shape=[_sds((b, l, d), F32), _sds((b, l, TOP_K), I32), _sds((b, l, TOP_K), F32)],
        compiler_params=_params(2), name="router",
    )(x, sc, sh, router_w, router_b.reshape(router_b.shape[0], 1, e))


def _moe_plan(idx_flat, tm, n_tiles):
    e = N_EXPERTS
    oh = (idx_flat[:, None] == jnp.arange(e, dtype=I32)[None, :]).astype(I32)
    csum = jnp.cumsum(oh, axis=0)
    counts = csum[-1]
    tiles_e = (counts + tm - 1) // tm
    tile_end = jnp.cumsum(tiles_e)
    row_start = (tile_end - tiles_e) * tm
    pos = jnp.sum(oh * (row_start[None, :] + csum - oh), axis=1).astype(I32)
    n_valid = tile_end[-1]
    tile_id = jnp.arange(n_tiles, dtype=I32)
    tile_row = jnp.minimum(tile_id, n_valid - 1)
    expert_of = lambda t: jnp.sum((t[:, None] >= tile_end[None, :]).astype(I32), axis=1).astype(I32)
    tile_e = expert_of(tile_row)
    prev_e = jnp.concatenate([jnp.full((1,), -1, I32), tile_e[:-1]])
    first = ((tile_id < n_valid) & (tile_e != prev_e)).astype(I32)
    run_end = tile_end[tile_e]
    nxt = jnp.where(run_end < n_valid, expert_of(jnp.minimum(run_end, n_valid - 1)), -1).astype(I32)
    grp = (jnp.cumsum(first) - 1).astype(I32)
    counts2 = jnp.stack([n_valid, jnp.sum(first)]).astype(I32)
    return pos, (tile_row, tile_e, first, nxt, grp, counts2)


def _dispatch_kernel(pos_ref, h_ref, xs_in_ref, xs_ref, sem, *, tt, base):
    del xs_in_ref
    t0 = pl.program_id(0) * tt

    def row_copy(t, r):
        return pltpu.make_async_copy(h_ref.at[pl.ds(t, 1), :], xs_ref.at[pl.ds(r, 1), :], sem)

    def issue(t, c):
        for k in range(TOP_K):
            row_copy(t, pos_ref[base + (t0 + t) * TOP_K + k]).start()
        return c

    lax.fori_loop(0, tt, issue, 0)

    def drain(t, c):
        for k in range(TOP_K):
            row_copy(0, 0).wait()
        return c

    lax.fori_loop(0, tt, drain, 0)


def _dispatch(pos, h2d, xs, base):
    t, d = h2d.shape
    tt = _row_tile(t, 256)
    return pl.pallas_call(
        functools.partial(_dispatch_kernel, tt=tt, base=base),
        grid_spec=pltpu.PrefetchScalarGridSpec(
            num_scalar_prefetch=1, grid=(t // tt,),
            in_specs=[pl.BlockSpec((tt, d), lambda i, pos: (i, 0)), pl.BlockSpec(memory_space=pl.ANY)],
            out_specs=pl.BlockSpec(memory_space=pl.ANY),
            scratch_shapes=[pltpu.SemaphoreType.DMA]),
        out_shape=_sds(xs.shape, xs.dtype), input_output_aliases={2: 0},
        compiler_params=_params(1), name="moe_dispatch")(pos, h2d, xs)


def _expert_weights(w_hbm, wbuf, wbf_ref, sem, sched, layer, col_blocks, tn):
    tile_e, first, nxt, grp, counts = sched
    n, r = pl.program_id(0), pl.program_id(1)
    slot = (n * counts[1] + grp[r]) % 2

    def copies(e, nn, s):
        return [pltpu.make_async_copy(
            w_hbm.at[layer, e, :, pl.ds(pl.multiple_of((nn + cb) * tn, tn), tn)], wbuf.at[s, c], sem.at[s])
            for c, cb in enumerate(col_blocks)]

    @pl.when((r < counts[0]) & (first[r] == 1))
    def _():
        @pl.when((n == 0) & (r == 0))
        def _():
            for cp in copies(tile_e[0], 0, 0):
                cp.start()

        for cp in copies(tile_e[r], n, slot):
            cp.wait()
        for c in range(len(col_blocks)):
            wbf_ref[c] = wbuf[slot, c].astype(BF16)

        @pl.when(nxt[r] >= 0)
        def _():
            for cp in copies(nxt[r], n, 1 - slot):
                cp.start()

        @pl.when((nxt[r] < 0) & (n + 1 < pl.num_programs(0)))
        def _():
            for cp in copies(tile_e[0], n + 1, 1 - slot):
                cp.start()


def _moe_up_kernel(tr_ref, te_ref, first_ref, nxt_ref, grp_ref, cnt_ref, x_ref, bg_ref, bl_ref, w_hbm,
                   o_ref, wbuf, wbf_ref, sem, *, layer, tn, nf):
    _expert_weights(w_hbm, wbuf, wbf_ref, sem, (te_ref, first_ref, nxt_ref, grp_ref, cnt_ref), layer, (0, nf), tn)

    @pl.when(pl.program_id(1) < cnt_ref[0])
    def _():
        x = x_ref[...].astype(BF16)
        g = jnp.dot(x, wbf_ref[0], preferred_element_type=F32) + bg_ref[0, 0]
        lin = jnp.dot(x, wbf_ref[1], preferred_element_type=F32) + bl_ref[0, 0]
        xg = jnp.minimum(g, SWIGLU_LIMIT)
        xl = jnp.clip(lin, -SWIGLU_LIMIT, SWIGLU_LIMIT)
        o_ref[...] = (xg * jax.nn.sigmoid(SWIGLU_ALPHA * xg) * (xl + 1.0)).astype(o_ref.dtype)


def _moe_down_kernel(tr_ref, te_ref, first_ref, nxt_ref, grp_ref, cnt_ref, a_ref, b_ref, w_hbm,
                     o_ref, wbuf, wbf_ref, sem, *, layer, tn):
    _expert_weights(w_hbm, wbuf, wbf_ref, sem, (te_ref, first_ref, nxt_ref, grp_ref, cnt_ref), layer, (0,), tn)

    @pl.when(pl.program_id(1) < cnt_ref[0])
    def _():
        o_ref[...] = jnp.dot(a_ref[...], wbf_ref[0], preferred_element_type=F32) + b_ref[0, 0]


def _moe_experts(xs, sched, w1, b1, w2, b2, layer):
    r_pad, d = xs.shape
    tm, tn = MOE_TM, MOE_TN_UP
    n_tiles = r_pad // tm
    f = w2.shape[2]
    nf = f // tn
    depth, e = b1.shape[0], b1.shape[1]
    b1r = b1.reshape(depth, e, 1, 2 * f)
    b2r = b2.reshape(depth, e, 1, d)
    ns = len(sched)
    imap = lambda fn: (lambda n, r, tr, te, *_: fn(n, r, tr, te))
    act = pl.pallas_call(
        functools.partial(_moe_up_kernel, layer=layer, tn=tn, nf=nf),
        grid_spec=pltpu.PrefetchScalarGridSpec(
            num_scalar_prefetch=ns, grid=(nf, n_tiles),
            in_specs=[pl.BlockSpec((tm, d), imap(lambda n, r, tr, te: (tr[r], 0))),
                      pl.BlockSpec((1, 1, 1, tn), imap(lambda n, r, tr, te: (layer, te[r], 0, n))),
                      pl.BlockSpec((1, 1, 1, tn), imap(lambda n, r, tr, te: (layer, te[r], 0, n + nf))),
                      pl.BlockSpec(memory_space=pl.ANY)],
            out_specs=pl.BlockSpec((tm, tn), imap(lambda n, r, tr, te: (tr[r], n))),
            scratch_shapes=[pltpu.VMEM((2, 2, d, tn), F32), pltpu.VMEM((2, d, tn), BF16),
                            pltpu.SemaphoreType.DMA((2,))]),
        out_shape=_sds((r_pad, f), BF16), compiler_params=_params(2), name="moe_up",
    )(*sched, xs, b1r, b1r, w1)
    tn = MOE_TN_DOWN
    nd = d // tn
    return pl.pallas_call(
        functools.partial(_moe_down_kernel, layer=layer, tn=tn),
        grid_spec=pltpu.PrefetchScalarGridSpec(
            num_scalar_prefetch=ns, grid=(nd, n_tiles),
            in_specs=[pl.BlockSpec((tm, f), imap(lambda n, r, tr, te: (tr[r], 0))),
                      pl.BlockSpec((1, 1, 1, tn), imap(lambda n, r, tr, te: (layer, te[r], 0, n))),
                      pl.BlockSpec(memory_space=pl.ANY)],
            out_specs=pl.BlockSpec((tm, tn), imap(lambda n, r, tr, te: (tr[r], n))),
            scratch_shapes=[pltpu.VMEM((2, 1, f, tn), F32), pltpu.VMEM((1, f, tn), BF16),
                            pltpu.SemaphoreType.DMA((2,))]),
        out_shape=_sds((r_pad, d), F32), compiler_params=_params(2), name="moe_down",
    )(*sched, act, b2r, w2)


def _combine_kernel(pos_ref, x_ref, g_ref, prob_ref, ys_ref, o_ref, buf_ref, sem, *, tt, base):
    t0 = (pl.program_id(0) * pl.num_programs(1) + pl.program_id(1)) * tt

    def row_copy(t, k, r):
        return pltpu.make_async_copy(ys_ref.at[pl.ds(r, 1), :], buf_ref.at[k, pl.ds(t, 1), :], sem)

    def issue(t, c):
        for k in range(TOP_K):
            row_copy(t, k, pos_ref[base + (t0 + t) * TOP_K + k]).start()
        return c

    lax.fori_loop(0, tt, issue, 0)

    def drain(t, c):
        for k in range(TOP_K):
            row_copy(0, 0, 0).wait()
        return c

    lax.fori_loop(0, tt, drain, 0)
    prob = prob_ref[0]
    acc = prob[:, 0:1] * buf_ref[0]
    for k in range(1, TOP_K):
        acc = acc + prob[:, k:k + 1] * buf_ref[k]
    o_ref[0] = x_ref[0] + g_ref[0] * acc


def _combine(pos, x, g, prob, ys, base):
    b, l, d = x.shape
    tt = _row_tile(l, 128)
    row = pl.BlockSpec((1, tt, d), lambda i, j, pos: (i, j, 0))
    return pl.pallas_call(
        functools.partial(_combine_kernel, tt=tt, base=base),
        grid_spec=pltpu.PrefetchScalarGridSpec(
            num_scalar_prefetch=1, grid=(b, l // tt),
            in_specs=[row, pl.BlockSpec((1, 1, d), lambda i, j, pos: (i, 0, 0)),
                      pl.BlockSpec((1, tt, TOP_K), lambda i, j, pos: (i, j, 0)),
                      pl.BlockSpec(memory_space=pl.ANY)],
            out_specs=row,
            scratch_shapes=[pltpu.VMEM((TOP_K, tt, d), F32), pltpu.SemaphoreType.DMA]),
        out_shape=_sds((b, l, d), F32), compiler_params=_params(2), name="moe_combine")(pos, x, g, prob, ys)


def _moe_rows(n_tokens):
    return (-(-n_tokens * TOP_K // MOE_TM) + N_EXPERTS) * MOE_TM


def _moe_layer(xs_groups, mods, xs, router_w, router_b, w1, b1, w2, b2, layer):
    d = xs_groups[0].shape[-1]
    routed = [_router(x, sc, sh, router_w, router_b, layer) for x, (sc, sh, _) in zip(xs_groups, mods)]
    idx_flat = jnp.concatenate([r[1].reshape(-1) for r in routed])
    pos, sched = _moe_plan(idx_flat, MOE_TM, xs.shape[0] // MOE_TM)
    bases, base = [], 0
    for h, _, _ in routed:
        bases.append(base)
        xs = _dispatch(pos, h.reshape(-1, d), xs, base)
        base += h.shape[0] * h.shape[1] * TOP_K
    ys = _moe_experts(xs, sched, w1, b1, w2, b2, layer)
    outs = [_combine(pos, x, g, r[2], ys, bs) for x, (_, _, g), r, bs in zip(xs_groups, mods, routed, bases)]
    return outs, xs


def kernel(x_prompt, x_sample, c_prompt, c_sample, cache_k, cache_v, state_conv, state_pool, page_table, ada_w, ada_b, w_in_even, w_in_odd, w_out, conv_w, conv_b, conv_ln_g, conv_ln_b, pool_w, pool_scale, router_w, router_b, moe_w1, moe_b1, moe_w2, moe_b2, final_g):
    depth = ada_w.shape[0]
    d = x_prompt.shape[-1]
    bp, lp, _ = x_prompt.shape
    bs, ls, _ = x_sample.shape
    past_len = page_table.shape[1] * cache_k.shape[2]
    pool_ch = pool_scale.shape[1]

    n_c = bp + bs
    c_all = jnp.pad(jnp.concatenate([c_prompt, c_sample], axis=0), ((0, -n_c % 8), (0, 0)))
    mod_all = _adaln(c_all, ada_w, ada_b)

    def mods(i, lo, hi):
        m = mod_all[i, lo:hi].reshape(hi - lo, 1, 6, d)
        return [m[:, :, n, :] for n in range(6)]

    xs = [x_prompt, x_sample]
    groups = [(0, bp), (bp, n_c)]
    conv_pre = [jnp.zeros((state_conv.shape[0], bp) + state_conv.shape[2:], F32), state_conv]
    pool_pre = [jnp.zeros((state_pool.shape[0], bp) + state_pool.shape[2:], F32), state_pool]
    starts = [0, past_len]
    new_k, new_v, new_conv, new_pool = [[], []], [[], []], [[], []], [[], []]
    moe_rows = jnp.zeros((_moe_rows(bp * lp + bs * ls), d), F32)

    for i in range(depth):
        j = i // 2
        mod = [mods(i, lo, hi) for lo, hi in groups]
        for gi in range(2):
            x = xs[gi]
            b, l, _ = x.shape
            sh1, sc1, g1 = mod[gi][0], mod[gi][1], mod[gi][2]
            h = _norm_mod(x, sc1, sh1)
            if i % 2 == 0:
                p = _mm(h, w_in_even, j)
                qkv0, tok0 = 0, 3 * ATTN_WIDTH
            else:
                p = _mm(h, w_in_odd, j)
                qkv0, tok0 = pool_ch, 0
            q, k, v = [p[..., qkv0 + n * ATTN_WIDTH:qkv0 + (n + 1) * ATTN_WIDTH] for n in range(3)]
            new_k[gi].append(k.reshape(b, l, N_HEADS, HEAD_DIM))
            new_v[gi].append(v.reshape(b, l, N_HEADS, HEAD_DIM))
            if gi == 0:
                o_att = _moba_prompt(p) if i % 2 == 0 else _sb_prompt(p, qkv0)
            else:
                attend = _moba_sample if i % 2 == 0 else _sb_sample
                o_att = attend(q, k, v, cache_k, cache_v, page_table, i).astype(BF16)
            if i % 2 == 0:
                o_tok, u = _conv_module(p, tok0, conv_pre[gi][j], conv_w[j], conv_b[j], conv_ln_g[j], conv_ln_b[j])
                full = jnp.concatenate([conv_pre[gi][j], u[:, -min(l, CONV_WIDTH - 1):]], axis=1)
                new_conv[gi].append(full[:, -(CONV_WIDTH - 1):])
            else:
                o_tok = _pool_module(p, pool_pre[gi][j], starts[gi], pool_w, j, pool_scale[j])
                n_keep = pool_pre[gi].shape[2]
                if gi == 0:
                    new_pool[gi].append(p[:, -n_keep:, :pool_ch])
                else:
                    full = jnp.concatenate([pool_pre[gi][j], p[..., :pool_ch]], axis=1)
                    new_pool[gi].append(full[:, -n_keep:])
            xs[gi] = _mm_out(o_att, o_tok, w_out, i, x, g1)
        moe_mods = [(m[4], m[3], m[5]) for m in mod]
        xs, moe_rows = _moe_layer(xs, moe_mods, moe_rows, router_w, router_b, moe_w1, moe_b1, moe_w2, moe_b2, i)

    y = [_final_norm(x, final_g) for x in xs]
    stack = lambda parts: jnp.stack(parts)
    return (y[0], y[1], stack(new_k[0]), stack(new_v[0]), stack(new_conv[0]), stack(new_pool[0]),
            stack(new_k[1]), stack(new_v[1]), stack(new_conv[1]), stack(new_pool[1]))
```

```python
import functools
import math

import jax
import jax.numpy as jnp
from jax import lax
from jax.experimental import pallas as pl
from jax.experimental.pallas import tpu as pltpu

F32 = jnp.float32
BF16 = jnp.bfloat16
I32 = jnp.int32
U32 = jnp.uint32

N_HEADS = 8
HEAD_DIM = 128
ATTN_WIDTH = N_HEADS * HEAD_DIM
MOBA_BLOCK = 256
MOBA_TOPK = 3
CONV_WIDTH = 31
CONV_HALO = 32
POOL_WINDOWS = (2, 4, 8, 16)
POOL_GROUP = 256
POOL_HALO = 16
N_EXPERTS = 32
TOP_K = 4
SWIGLU_LIMIT = 7.0
SWIGLU_ALPHA = 1.702
EPS = 1e-6
ATTN_SCALE = 1.0 / math.sqrt(HEAD_DIM)

VMEM_LIMIT_V7X = 56 * 1024 * 1024
MOE_TM = 512
MOE_ROW_STEP = 128
MOE_TN_UP = 512
MOE_TN_DOWN = 1024
HEADS_PER_STEP = 2
QPAD = 8
PAGES_PER_STEP = 4
NT_DIMS = (((1,), (1,)), ((), ()))


def _params(n_axes):
    return pltpu.CompilerParams(dimension_semantics=("arbitrary",) * n_axes,
                                vmem_limit_bytes=VMEM_LIMIT_V7X)


def _sds(shape, dtype):
    return jax.ShapeDtypeStruct(shape, dtype)


def _adaln_kernel(c_ref, w_ref, b_ref, o_ref):
    c = c_ref[...]
    a = (c * jax.nn.sigmoid(c)).astype(BF16)
    o_ref[0] = jnp.dot(a, w_ref[0].astype(BF16), preferred_element_type=F32) + b_ref[0]


def _adaln(c_all, ada_w, ada_b):
    depth, d, n = ada_w.shape
    r = c_all.shape[0]
    tn = 1024
    return pl.pallas_call(
        _adaln_kernel, grid=(depth, n // tn),
        in_specs=[pl.BlockSpec((r, d), lambda i, j: (0, 0)),
                  pl.BlockSpec((1, d, tn), lambda i, j: (i, 0, j)),
                  pl.BlockSpec((1, 1, tn), lambda i, j: (i, 0, j))],
        out_specs=pl.BlockSpec((1, r, tn), lambda i, j: (i, 0, j)),
        out_shape=_sds((depth, r, n), F32), compiler_params=_params(2), name="adaln",
    )(c_all, ada_w, ada_b.reshape(depth, 1, n))


def _rms(x):
    return x * lax.rsqrt(jnp.mean(x * x, axis=-1, keepdims=True) + EPS)


def _norm_mod_kernel(x_ref, sc_ref, sh_ref, o_ref):
    o_ref[0] = (_rms(x_ref[0]) * (1.0 + sc_ref[0]) + sh_ref[0]).astype(o_ref.dtype)


def _row_tile(l, cap):
    return l if l <= cap else cap


def _col_tile(l):
    return 512 if l >= 512 else 1024


def _norm_mod(x, sc, sh):
    b, l, d = x.shape
    tl = _row_tile(l, 512)
    row = pl.BlockSpec((1, tl, d), lambda i, j: (i, j, 0))
    per_b = pl.BlockSpec((1, 1, d), lambda i, j: (i, 0, 0))
    return pl.pallas_call(_norm_mod_kernel, grid=(b, l // tl), in_specs=[row, per_b, per_b], out_specs=row,
                          out_shape=_sds((b, l, d), BF16), compiler_params=_params(2), name="norm_mod")(x, sc, sh)


def _final_norm_kernel(x_ref, g_ref, o_ref):
    o_ref[0] = _rms(x_ref[0]) * g_ref[...]


def _final_norm(x, g):
    b, l, d = x.shape
    tl = _row_tile(l, 512)
    row = pl.BlockSpec((1, tl, d), lambda i, j: (i, j, 0))
    return pl.pallas_call(_final_norm_kernel, grid=(b, l // tl),
                          in_specs=[row, pl.BlockSpec((1, d), lambda i, j: (0, 0))], out_specs=row,
                          out_shape=_sds((b, l, d), F32), compiler_params=_params(2), name="final_norm")(x, g.reshape(1, d))


def _first_row_tile():
    return (pl.program_id(1) == 0) & (pl.program_id(2) == 0)


def _mm_kernel(a_ref, w_ref, o_ref, wbf_ref):
    @pl.when(_first_row_tile())
    def _():
        wbf_ref[...] = w_ref[0].astype(BF16)

    o_ref[0] = jnp.dot(a_ref[0], wbf_ref[...], preferred_element_type=F32)


def _mm(a, w, layer):
    b, l, k = a.shape
    n = w.shape[2]
    tl, tn = _row_tile(l, 512), _col_tile(l)
    return pl.pallas_call(
        _mm_kernel, grid=(n // tn, b, l // tl),
        in_specs=[pl.BlockSpec((1, tl, k), lambda j, bi, li: (bi, li, 0)),
                  pl.BlockSpec((1, k, tn), lambda j, bi, li: (layer, 0, j))],
        out_specs=pl.BlockSpec((1, tl, tn), lambda j, bi, li: (bi, li, j)),
        out_shape=_sds((b, l, n), F32), scratch_shapes=[pltpu.VMEM((k, tn), BF16)],
        compiler_params=_params(3), name="in_proj")(a, w)


def _mm_out_kernel(a1_ref, a2_ref, w1_ref, w2_ref, x_ref, g_ref, o_ref, w1bf_ref, w2bf_ref):
    @pl.when(_first_row_tile())
    def _():
        w1bf_ref[...] = w1_ref[0].astype(BF16)
        w2bf_ref[...] = w2_ref[0].astype(BF16)

    y = jnp.dot(a1_ref[0], w1bf_ref[...], preferred_element_type=F32)
    y = y + jnp.dot(a2_ref[0], w2bf_ref[...], preferred_element_type=F32)
    o_ref[0] = x_ref[0] + g_ref[0] * y


def _mm_out(a1, a2, w, layer, x, g):
    b, l, kh = a1.shape
    n = w.shape[2]
    tl, tn = _row_tile(l, 512), _col_tile(l)
    g_spec = (pl.BlockSpec((1, 1, tn), lambda j, bi, li: (bi, 0, j)) if g.shape[1] == 1
              else pl.BlockSpec((1, tl, tn), lambda j, bi, li: (bi, li, j)))
    a_spec = pl.BlockSpec((1, tl, kh), lambda j, bi, li: (bi, li, 0))
    return pl.pallas_call(
        _mm_out_kernel, grid=(n // tn, b, l // tl),
        in_specs=[a_spec, a_spec,
                  pl.BlockSpec((1, kh, tn), lambda j, bi, li: (layer, 0, j)),
                  pl.BlockSpec((1, kh, tn), lambda j, bi, li: (layer, 1, j)),
                  pl.BlockSpec((1, tl, tn), lambda j, bi, li: (bi, li, j)),
                  g_spec],
        out_specs=pl.BlockSpec((1, tl, tn), lambda j, bi, li: (bi, li, j)),
        out_shape=_sds((b, l, n), F32),
        scratch_shapes=[pltpu.VMEM((kh, tn), BF16), pltpu.VMEM((kh, tn), BF16)],
        compiler_params=_params(3), name="out_proj")(a1, a2, w, w, x, g)


def _block_rank(s, idx, n, axis):
    rank = jnp.zeros(s.shape, I32)
    for j2 in range(n):
        c = s[:, j2:j2 + 1] if axis == 1 else s[j2:j2 + 1, :]
        rank = rank + ((c > s) | ((c == s) & (j2 < idx))).astype(I32)
    return rank


def _head_cols(h):
    return slice(h * HEAD_DIM, (h + 1) * HEAD_DIM)


def _kmean_kernel(k_ref, o_ref):
    n = pl.program_id(1)
    o_ref[0, pl.ds(n, 1), :] = jnp.mean(k_ref[0], axis=0, keepdims=True)


def _moba_prompt_kernel(q_ref, k_ref, v_ref, km_ref, sl_ref, o_ref):
    i = pl.program_id(2)
    t = MOBA_BLOCK
    nb = km_ref.shape[1]
    nh = q_ref.shape[2] // HEAD_DIM
    rowi = lax.broadcasted_iota(I32, (t, t), 0)
    coli = lax.broadcasted_iota(I32, (t, t), 1)
    eye = (rowi == coli).astype(BF16)
    blk_t = lax.broadcasted_iota(I32, (nb, t), 0)
    blk = lax.broadcasted_iota(I32, (t, nb), 1)
    qbs, sels, slopes = [], [], []
    for h in range(nh):
        q = q_ref[0, :, _head_cols(h)]
        s_t = lax.dot_general(km_ref[0, :, _head_cols(h)], q, NT_DIMS,
                              precision=lax.Precision.HIGHEST, preferred_element_type=F32)
        s_t = jnp.where(blk_t < i, s_t, -jnp.inf)
        sel_t = ((blk_t < i) & (_block_rank(s_t, blk_t, nb, 0) < MOBA_TOPK)).astype(BF16)
        sels.append(lax.dot_general(eye, sel_t, NT_DIMS, preferred_element_type=F32))
        qbs.append(q.astype(BF16))
        slopes.append(sl_ref[h][:, :1])

    def attend(h, j, bias, carry):
        m, l, acc = carry
        rows = pl.ds(pl.multiple_of(j * t, t), t)
        kj = k_ref[0, rows, _head_cols(h)].astype(BF16)
        vj = v_ref[0, rows, _head_cols(h)].astype(BF16)
        lg = lax.dot_general(qbs[h], kj, NT_DIMS, preferred_element_type=F32) * ATTN_SCALE
        lg = lg - slopes[h] * ((i - j) * t + rowi - coli).astype(F32) + bias
        m_new = jnp.maximum(m, jnp.max(lg, axis=1, keepdims=True))
        alpha = jnp.exp(m - m_new)
        p = jnp.exp(lg - m_new)
        l = alpha * l + jnp.sum(p, axis=1, keepdims=True)
        acc = alpha * acc + jnp.dot(p.astype(BF16), vj, preferred_element_type=F32)
        return m_new, l, acc

    init = (jnp.full((t, 1), -jnp.inf, F32), jnp.zeros((t, 1), F32), jnp.zeros((t, HEAD_DIM), F32))
    causal = jnp.where(coli <= rowi, 0.0, -jnp.inf)
    carry = tuple(attend(h, i, causal, init) for h in range(nh))

    def body(j, carry):
        out = []
        for h in range(nh):
            picked = jnp.max(jnp.where(blk == j, sels[h], 0.0), axis=1, keepdims=True)
            out.append(attend(h, j, jnp.where(picked > 0.5, 0.0, -jnp.inf), carry[h]))
        return tuple(out)

    carry = lax.fori_loop(0, i, body, carry)
    for h in range(nh):
        _, l, acc = carry[h]
        o_ref[0, :, _head_cols(h)] = (acc / l).astype(o_ref.dtype)


def _alibi_slopes():
    return jnp.asarray([2.0 ** (-8.0 * (h + 1) / N_HEADS) for h in range(N_HEADS)], F32)


def _moba_prompt(p):
    b, l, _ = p.shape
    t = MOBA_BLOCK
    nb = l // t
    kmean = pl.pallas_call(
        _kmean_kernel, grid=(b, nb),
        in_specs=[pl.BlockSpec((1, t, ATTN_WIDTH), lambda bi, n: (bi, n, 1))],
        out_specs=pl.BlockSpec((1, nb, ATTN_WIDTH), lambda bi, n: (bi, 0, 0)),
        out_shape=_sds((b, nb, ATTN_WIDTH), F32), compiler_params=_params(2), name="moba_kmean")(p)
    nh = HEADS_PER_STEP
    w = nh * HEAD_DIM
    hg = N_HEADS // nh
    slopes = jnp.broadcast_to(_alibi_slopes()[:, None, None], (N_HEADS, 1, HEAD_DIM))
    return pl.pallas_call(
        _moba_prompt_kernel, grid=(b, hg, nb),
        in_specs=[pl.BlockSpec((1, t, w), lambda bi, hi, i: (bi, i, hi)),
                  pl.BlockSpec((1, l, w), lambda bi, hi, i: (bi, 0, hg + hi)),
                  pl.BlockSpec((1, l, w), lambda bi, hi, i: (bi, 0, 2 * hg + hi)),
                  pl.BlockSpec((1, nb, w), lambda bi, hi, i: (bi, 0, hi)),
                  pl.BlockSpec((nh, 1, HEAD_DIM), lambda bi, hi, i: (hi, 0, 0))],
        out_specs=pl.BlockSpec((1, t, w), lambda bi, hi, i: (bi, i, hi)),
        out_shape=_sds((b, l, ATTN_WIDTH), BF16), compiler_params=_params(3), name="moba_prompt",
    )(p, p, p, kmean, slopes)


def _log_sigmoid_pair(z):
    t = jnp.log(1.0 + jnp.exp(-jnp.abs(z)))
    return jnp.minimum(z, 0.0) - t, jnp.minimum(-z, 0.0) - t


def _split_bf16(x):
    hi = x.astype(BF16)
    return hi, (x - hi.astype(F32)).astype(BF16)


def _suffix_sums(lk, later):
    hi, lo = _split_bf16(lk)
    return jnp.dot(hi, later, preferred_element_type=F32) + jnp.dot(lo, later, preferred_element_type=F32)


def _sb_prompt_kernel(q_ref, k_ref, v_ref, o_ref):
    i = pl.program_id(2)
    t = q_ref.shape[1]
    nh = q_ref.shape[2] // HEAD_DIM
    qbs = [q_ref[0, :, _head_cols(h)].astype(BF16) for h in range(nh)]
    rowi = lax.broadcasted_iota(I32, (t, t), 0)
    coli = lax.broadcasted_iota(I32, (t, t), 1)
    later = (rowi > coli).astype(BF16)

    def block(j, mask, carry):
        rows = pl.ds(pl.multiple_of(j * t, t), t)
        out = []
        for h in range(nh):
            c, acc = carry[h]
            kj = k_ref[0, rows, _head_cols(h)].astype(BF16)
            vj = v_ref[0, rows, _head_cols(h)].astype(BF16)
            z = lax.dot_general(qbs[h], kj, NT_DIMS, preferred_element_type=F32) * ATTN_SCALE
            ls_pos, ls_neg = _log_sigmoid_pair(z)
            lk = ls_neg if mask is None else jnp.where(mask, ls_neg, 0.0)
            a = jnp.exp(ls_pos + _suffix_sums(lk, later) + c)
            if mask is not None:
                a = jnp.where(mask, a, 0.0)
            acc = acc + jnp.dot(a.astype(BF16), vj, preferred_element_type=F32)
            out.append((c + jnp.sum(lk, axis=1, keepdims=True), acc))
        return tuple(out)

    init = tuple((jnp.zeros((t, 1), F32), jnp.zeros((t, HEAD_DIM), F32)) for _ in range(nh))
    carry = block(i, coli < rowi, init)
    carry = lax.fori_loop(1, i + 1, lambda step, carry: block(i - step, None, carry), carry)
    for h in range(nh):
        o_ref[0, :, _head_cols(h)] = carry[h][1].astype(o_ref.dtype)


def _sb_prompt(p, col0):
    b, l, _ = p.shape
    t = 256
    nh = HEADS_PER_STEP
    w = nh * HEAD_DIM
    hg = N_HEADS // nh
    c0 = col0 // w
    return pl.pallas_call(
        _sb_prompt_kernel, grid=(b, hg, l // t),
        in_specs=[pl.BlockSpec((1, t, w), lambda bi, hi, i: (bi, i, c0 + hi)),
                  pl.BlockSpec((1, l, w), lambda bi, hi, i: (bi, 0, c0 + hg + hi)),
                  pl.BlockSpec((1, l, w), lambda bi, hi, i: (bi, 0, c0 + 2 * hg + hi))],
        out_specs=pl.BlockSpec((1, t, w), lambda bi, hi, i: (bi, i, hi)),
        out_shape=_sds((b, l, ATTN_WIDTH), BF16), compiler_params=_params(3), name="sb_prompt")(p, p, p)


def _head_rows(ref, h):
    return ref[0, 0, pl.ds(h, ref.shape[2] // N_HEADS, stride=N_HEADS), :]


def _q_rows(q_ref, h):
    return q_ref[0, h * QPAD:(h + 1) * QPAD, :]


def _qk_scores(q_ref, k_ref, with_key_sums=False):
    zs, sums = [], []
    for h in range(N_HEADS):
        kh = _head_rows(k_ref, h)
        zs.append(lax.dot_general(_q_rows(q_ref, h).astype(BF16), kh.astype(BF16), NT_DIMS,
                                  preferred_element_type=F32))
        if with_key_sums:
            sums.append(jnp.sum(kh, axis=0, keepdims=True))
    z = jnp.concatenate(zs, axis=0) * ATTN_SCALE
    return (z, sums) if with_key_sums else z


def _pv(a, v_ref):
    outs = []
    for h in range(N_HEADS):
        ah = a[h * QPAD:(h + 1) * QPAD, :].astype(BF16)
        outs.append(jnp.dot(ah, _head_rows(v_ref, h).astype(BF16), preferred_element_type=F32))
    return jnp.concatenate(outs, axis=0)


def _query_index(shape):
    return lax.broadcasted_iota(I32, shape, 0) & (QPAD - 1)


def _sb_page(q_ref, k_ref, v_ref, mask, later, c):
    z = _qk_scores(q_ref, k_ref)
    ls_pos, ls_neg = _log_sigmoid_pair(z)
    lk = ls_neg if mask is None else jnp.where(mask, ls_neg, 0.0)
    a = jnp.exp(ls_pos + _suffix_sums(lk, later) + c)
    if mask is not None:
        a = jnp.where(mask, a, 0.0)
    return _pv(a, v_ref), c + jnp.sum(lk, axis=1, keepdims=True)


def _sb_sample_kernel(pt_ref, q_ref, kn_ref, vn_ref, *refs):
    del pt_ref
    pp = PAGES_PER_STEP
    k_refs, v_refs = refs[:pp], refs[pp:2 * pp]
    o_ref, c_ref = refs[2 * pp], refs[2 * pp + 1]
    rows = q_ref.shape[1]
    page = k_refs[0].shape[2] // N_HEADS
    later = (lax.broadcasted_iota(I32, (page, page), 0) > lax.broadcasted_iota(I32, (page, page), 1)).astype(BF16)

    @pl.when(pl.program_id(1) == 0)
    def _():
        mask = lax.broadcasted_iota(I32, (rows, page), 1) < _query_index((rows, page))
        o, c = _sb_page(q_ref, kn_ref, vn_ref, mask, later, jnp.zeros((rows, 1), F32))
        o_ref[0] = o
        c_ref[...] = c

    acc, c = o_ref[0], c_ref[...]
    for k_ref, v_ref in zip(k_refs, v_refs):
        o, c = _sb_page(q_ref, k_ref, v_ref, None, later, c)
        acc = acc + o
    o_ref[0] = acc
    c_ref[...] = c


def _stack_queries(q):
    b, lq, _ = q.shape
    assert lq <= QPAD
    q = q.reshape(b, lq, N_HEADS, HEAD_DIM).transpose(0, 2, 1, 3)
    return jnp.pad(q, ((0, 0), (0, 0), (0, QPAD - lq), (0, 0))).reshape(b, N_HEADS * QPAD, HEAD_DIM)


def _unstack_queries(o, lq):
    b = o.shape[0]
    o = o.reshape(b, N_HEADS, QPAD, HEAD_DIM)[:, :, :lq]
    return o.transpose(0, 2, 1, 3).reshape(b, lq, ATTN_WIDTH)


def _new_page(x, page):
    b, lq, _ = x.shape
    return jnp.pad(x, ((0, 0), (0, page - lq), (0, 0))).reshape(b, 1, page * N_HEADS, HEAD_DIM)


def _page_view(cache):
    depth, n_pool, page, h, d = cache.shape
    return cache.reshape(depth, n_pool, page * h, d)


def _sb_sample(q, k_new, v_new, cache_k, cache_v, page_table, layer):
    b, lq, _ = q.shape
    n_pages = page_table.shape[1]
    page = cache_k.shape[2]
    pp = PAGES_PER_STEP
    assert n_pages % pp == 0
    rows = N_HEADS * QPAD
    d = HEAD_DIM
    per_b = pl.BlockSpec((1, rows, d), lambda bi, s, pt: (bi, 0, 0))
    new = pl.BlockSpec((1, 1, page * N_HEADS, d), lambda bi, s, pt: (bi, 0, 0, 0))
    pg = lambda j: pl.BlockSpec((1, 1, page * N_HEADS, d),
                                lambda bi, s, pt: (layer, pt[bi, n_pages - 1 - (s * pp + j)], 0, 0))
    ck, cv = _page_view(cache_k), _page_view(cache_v)
    o = pl.pallas_call(
        _sb_sample_kernel,
        grid_spec=pltpu.PrefetchScalarGridSpec(
            num_scalar_prefetch=1, grid=(b, n_pages // pp),
            in_specs=[per_b, new, new] + [pg(j) for j in range(pp)] * 2,
            out_specs=per_b, scratch_shapes=[pltpu.VMEM((rows, 1), F32)]),
        out_shape=_sds((b, rows, d), F32), compiler_params=_params(2), name="sb_sample",
    )(page_table, _stack_queries(q), _new_page(k_new, page), _new_page(v_new, page), *([ck] * pp), *([cv] * pp))
    return _unstack_queries(o, lq)


def _moba_sample_part_kernel(pt_ref, q_ref, sl_ref, *refs, past_len):
    del pt_ref
    pp = PAGES_PER_STEP
    k_refs, v_refs = refs[:pp], refs[pp:2 * pp]
    o_ref, st_ref = refs[2 * pp], refs[2 * pp + 1]
    rows = q_ref.shape[1]
    page = k_refs[0].shape[2] // N_HEADS
    per_blk = MOBA_BLOCK // page
    qpos = past_len + _query_index((rows, page))
    lane = lax.broadcasted_iota(I32, (rows, page), 1)
    stat_lane = lax.broadcasted_iota(I32, (rows, HEAD_DIM), 1)
    for blk in range(pp // per_blk):
        n = pl.program_id(1) * (pp // per_blk) + blk
        lgs, key_sums = [], None
        for j in range(per_blk):
            z, sums = _qk_scores(q_ref, k_refs[blk * per_blk + j], with_key_sums=True)
            kpos = n * MOBA_BLOCK + j * page + lane
            lgs.append(z - sl_ref[...] * (qpos - kpos).astype(F32))
            key_sums = sums if key_sums is None else [a + b for a, b in zip(key_sums, sums)]
        m = functools.reduce(jnp.maximum, [jnp.max(lg, axis=1, keepdims=True) for lg in lgs])
        pes = [jnp.exp(lg - m) for lg in lgs]
        l = functools.reduce(jnp.add, [jnp.sum(pe, axis=1, keepdims=True) for pe in pes])
        o = functools.reduce(jnp.add, [_pv(pe, v_refs[blk * per_blk + j]) for j, pe in enumerate(pes)])
        s = jnp.concatenate([jnp.sum(_q_rows(q_ref, h) * (key_sums[h] * (1.0 / MOBA_BLOCK)), axis=1, keepdims=True)
                             for h in range(N_HEADS)], axis=0)
        o_ref[0, blk] = o
        st_ref[0, blk] = jnp.where(stat_lane == 0, m, jnp.where(stat_lane == 1, l, jnp.where(stat_lane == 2, s, 0.0)))


def _moba_sample_merge_kernel(q_ref, sl_ref, kn_ref, vn_ref, op_ref, m_ref, l_ref, s_ref, o_ref):
    rows = q_ref.shape[1]
    m, l, s = m_ref[0], l_ref[0], s_ref[0]
    nb = m.shape[1]
    sel = _block_rank(s, lax.broadcasted_iota(I32, (rows, nb), 1), nb, 1) < MOBA_TOPK
    z = _qk_scores(q_ref, kn_ref)
    dist = _query_index(z.shape) - lax.broadcasted_iota(I32, z.shape, 1)
    lg = jnp.where(dist >= 0, z - sl_ref[...] * dist.astype(F32), -jnp.inf)
    m_all = jnp.maximum(jnp.max(lg, axis=1, keepdims=True),
                        jnp.max(jnp.where(sel, m, -jnp.inf), axis=1, keepdims=True))
    pe = jnp.exp(lg - m_all)
    w = jnp.where(sel, jnp.exp(m - m_all), 0.0)
    denom = jnp.sum(pe, axis=1, keepdims=True) + jnp.sum(w * l, axis=1, keepdims=True)
    acc = _pv(pe, vn_ref)
    for n in range(nb):
        acc = acc + w[:, n:n + 1] * op_ref[0, n]
    o_ref[0] = acc / denom


def _moba_sample(q, k_new, v_new, cache_k, cache_v, page_table, layer):
    b, lq, _ = q.shape
    n_pages = page_table.shape[1]
    page = cache_k.shape[2]
    pp = PAGES_PER_STEP
    per_blk = MOBA_BLOCK // page
    assert MOBA_BLOCK % page == 0 and pp % per_blk == 0 and n_pages % pp == 0
    nb = n_pages // per_blk
    bps = pp // per_blk
    past_len = n_pages * page
    rows = N_HEADS * QPAD
    d = HEAD_DIM
    slopes = jnp.repeat(_alibi_slopes(), QPAD)[:, None]
    qs = _stack_queries(q)
    ck, cv = _page_view(cache_k), _page_view(cache_v)
    pg = lambda j: pl.BlockSpec((1, 1, page * N_HEADS, d), lambda bi, s, pt: (layer, pt[bi, s * pp + j], 0, 0))
    part = pl.BlockSpec((1, bps, rows, d), lambda bi, s, pt: (bi, s, 0, 0))
    o_part, stats = pl.pallas_call(
        functools.partial(_moba_sample_part_kernel, past_len=past_len),
        grid_spec=pltpu.PrefetchScalarGridSpec(
            num_scalar_prefetch=1, grid=(b, n_pages // pp),
            in_specs=[pl.BlockSpec((1, rows, d), lambda bi, s, pt: (bi, 0, 0)),
                      pl.BlockSpec((rows, 1), lambda bi, s, pt: (0, 0))] + [pg(j) for j in range(pp)] * 2,
            out_specs=[part, part]),
        out_shape=[_sds((b, nb, rows, d), F32)] * 2, compiler_params=_params(2), name="moba_sample_part",
    )(page_table, qs, slopes, *([ck] * pp), *([cv] * pp))
    m, l, s = [jnp.swapaxes(stats[..., c], 1, 2) for c in range(3)]
    per_b = pl.BlockSpec((1, rows, d), lambda bi: (bi, 0, 0))
    new = pl.BlockSpec((1, 1, page * N_HEADS, d), lambda bi: (bi, 0, 0, 0))
    stat = pl.BlockSpec((1, rows, nb), lambda bi: (bi, 0, 0))
    o = pl.pallas_call(
        _moba_sample_merge_kernel, grid=(b,),
        in_specs=[per_b, pl.BlockSpec((rows, 1), lambda bi: (0, 0)), new, new,
                  pl.BlockSpec((1, nb, rows, d), lambda bi: (bi, 0, 0, 0)), stat, stat, stat],
        out_specs=per_b, out_shape=_sds((b, rows, d), F32), compiler_params=_params(1), name="moba_sample_merge",
    )(qs, slopes, _new_page(k_new, page), _new_page(v_new, page), o_part, m, l, s)
    return _unstack_queries(o, lq)


def _conv_kernel(*refs, tl, use_halo):
    if use_halo:
        ga_ref, gb_ref, hga_ref, hgb_ref, pre_ref, w_ref, b_ref, g_ref, bt_ref, o_ref, u_ref, ext_ref, y_ref = refs
    else:
        ga_ref, gb_ref, pre_ref, w_ref, b_ref, g_ref, bt_ref, o_ref, u_ref, ext_ref, y_ref = refs
    i = pl.program_id(1)
    u = ga_ref[0] * jax.nn.sigmoid(gb_ref[0])
    u_ref[0] = u
    ext_ref[CONV_HALO:CONV_HALO + tl, :] = u

    @pl.when(i == 0)
    def _():
        ext_ref[0:CONV_HALO, :] = pre_ref[0]

    if use_halo:
        @pl.when(i > 0)
        def _():
            ext_ref[0:CONV_HALO, :] = hga_ref[0] * jax.nn.sigmoid(hgb_ref[0])

    lane = 128
    off = CONV_HALO - (CONV_WIDTH - 1)

    def chunk(c, carry):
        cs = pl.ds(pl.multiple_of(c * lane, lane), lane)
        acc = jnp.zeros((tl, lane), F32)
        for j in range(CONV_WIDTH):
            acc = acc + ext_ref[off + j:off + j + tl, cs] * w_ref[j:j + 1, cs]
        y_ref[:, cs] = acc
        return carry

    lax.fori_loop(0, ext_ref.shape[1] // lane, chunk, 0)
    y = y_ref[...] + b_ref[...]
    mu = jnp.mean(y, axis=-1, keepdims=True)
    var = jnp.mean(jnp.square(y - mu), axis=-1, keepdims=True)
    yn = (y - mu) * lax.rsqrt(var + EPS) * g_ref[...] + bt_ref[...]
    o_ref[0] = (yn * jax.nn.sigmoid(yn)).astype(o_ref.dtype)


def _conv_module(p, col0, prefix, w, bias, ln_g, ln_b):
    b, l, _ = p.shape
    c = w.shape[1]
    tl = _row_tile(l, 256)
    use_halo = l > tl
    ca, cb = col0 // c, col0 // c + 1
    pre = jnp.pad(prefix, ((0, 0), (CONV_HALO - prefix.shape[1], 0), (0, 0)))
    wp = jnp.pad(w, ((0, CONV_HALO - w.shape[0]), (0, 0)))
    row = lambda cc: pl.BlockSpec((1, tl, c), lambda bi, i: (bi, i, cc))
    hr = tl // CONV_HALO
    halo = lambda cc: pl.BlockSpec((1, CONV_HALO, c), lambda bi, i: (bi, jnp.maximum(i * hr - 1, 0), cc))
    vec = pl.BlockSpec((1, c), lambda bi, i: (0, 0))
    in_specs = [row(ca), row(cb)] + ([halo(ca), halo(cb)] if use_halo else []) + [
        pl.BlockSpec((1, CONV_HALO, c), lambda bi, i: (bi, 0, 0)),
        pl.BlockSpec((CONV_HALO, c), lambda bi, i: (0, 0)), vec, vec, vec]
    args = [p, p] + ([p, p] if use_halo else []) + [pre, wp, bias.reshape(1, c), ln_g.reshape(1, c), ln_b.reshape(1, c)]
    out_row = pl.BlockSpec((1, tl, c), lambda bi, i: (bi, i, 0))
    return pl.pallas_call(
        functools.partial(_conv_kernel, tl=tl, use_halo=use_halo), grid=(b, l // tl),
        in_specs=in_specs, out_specs=[out_row, out_row],
        out_shape=[_sds((b, l, c), BF16), _sds((b, l, c), F32)],
        scratch_shapes=[pltpu.VMEM((CONV_HALO + tl, c), F32), pltpu.VMEM((tl, c), F32)],
        compiler_params=_params(2), name="conv_module")(*args)


def _pool_kernel(*refs, tl, start, use_halo):
    if use_halo:
        u_ref, hu_ref, pre_ref, w_ref, sc_ref, o_ref, ext_ref = refs
    else:
        u_ref, pre_ref, w_ref, sc_ref, o_ref, ext_ref = refs
    i = pl.program_id(1)
    ext_ref[POOL_HALO:POOL_HALO + tl, :] = u_ref[0]

    @pl.when(i == 0)
    def _():
        ext_ref[0:POOL_HALO, :] = pre_ref[0]

    if use_halo:
        @pl.when(i > 0)
        def _():
            ext_ref[0:POOL_HALO, :] = hu_ref[0]

    pos1 = start + i * tl + lax.broadcasted_iota(I32, (tl, 1), 0) + 1
    for g, win in enumerate(POOL_WINDOWS):
        cs = slice(g * POOL_GROUP, (g + 1) * POOL_GROUP)
        ug = ext_ref[POOL_HALO:POOL_HALO + tl, cs]
        ws = ug
        for dlt in range(1, win):
            ws = ws + ext_ref[POOL_HALO - dlt:POOL_HALO - dlt + tl, cs]
        cnt = jnp.minimum(pos1, win).astype(F32)
        delta = ws / cnt - ug
        mixed = jnp.dot(delta.astype(BF16), w_ref[0, g].astype(BF16), preferred_element_type=F32)
        o_ref[0, :, cs] = (mixed * sc_ref[:, cs]).astype(o_ref.dtype)


def _pool_module(p, prefix, start, w_all, layer_j, scale):
    b, l, _ = p.shape
    c = scale.shape[0]
    tl = _row_tile(l, 256)
    use_halo = l > tl
    pre = jnp.pad(prefix, ((0, 0), (POOL_HALO - prefix.shape[1], 0), (0, 0)))
    row = pl.BlockSpec((1, tl, c), lambda bi, i: (bi, i, 0))
    hr = tl // POOL_HALO
    halo = pl.BlockSpec((1, POOL_HALO, c), lambda bi, i: (bi, jnp.maximum(i * hr - 1, 0), 0))
    in_specs = [row] + ([halo] if use_halo else []) + [
        pl.BlockSpec((1, POOL_HALO, c), lambda bi, i: (bi, 0, 0)),
        pl.BlockSpec((1,) + w_all.shape[1:], lambda bi, i: (layer_j, 0, 0, 0)),
        pl.BlockSpec((1, c), lambda bi, i: (0, 0))]
    args = [p] + ([p] if use_halo else []) + [pre, w_all, scale.reshape(1, c)]
    return pl.pallas_call(
        functools.partial(_pool_kernel, tl=tl, start=start, use_halo=use_halo), grid=(b, l // tl),
        in_specs=in_specs, out_specs=row, out_shape=_sds((b, l, c), BF16),
        scratch_shapes=[pltpu.VMEM((POOL_HALO + tl, c), F32)],
        compiler_params=_params(2), name="pool_module")(*args)


def _router_kernel(x_ref, sc_ref, sh_ref, rw_ref, rb_ref, h_ref, idx_ref, prob_ref):
    h = _rms(x_ref[0]) * (1.0 + sc_ref[0]) + sh_ref[0]
    half = h.shape[1] // 2
    h_ref[0] = _pack_bf16_pair(h[:, :half], h[:, half:])
    lg = jnp.dot(h, rw_ref[0], precision=lax.Precision.HIGHEST, preferred_element_type=F32) + rb_ref[0]
    lane = lax.broadcasted_iota(I32, lg.shape, 1)
    idxs, vals = [], []
    for _ in range(TOP_K):
        m = jnp.max(lg, axis=1, keepdims=True)
        ix = jnp.min(jnp.where(lg == m, lane, N_EXPERTS), axis=1, keepdims=True)
        idxs.append(ix)
        vals.append(m)
        lg = jnp.where(lane == ix, -jnp.inf, lg)
    es = [jnp.exp(v - vals[0]) for v in vals]
    tot = es[0] + es[1] + es[2] + es[3]
    idx_ref[0] = jnp.concatenate(idxs, axis=1)
    prob_ref[0] = jnp.concatenate([e / tot for e in es], axis=1)


def _router(x, sc, sh, router_w, router_b, layer):
    b, l, d = x.shape
    e = router_w.shape[2]
    tl = _row_tile(l, 256)
    row = pl.BlockSpec((1, tl, d), lambda i, j: (i, j, 0))
    packed = pl.BlockSpec((1, tl, d // 2), lambda i, j: (i, j, 0))
    per_b = pl.BlockSpec((1, 1, d), lambda i, j: (i, 0, 0))
    top = pl.BlockSpec((1, tl, TOP_K), lambda i, j: (i, j, 0))
    return pl.pallas_call(
        _router_kernel, grid=(b, l // tl),
        in_specs=[row, per_b, per_b, pl.BlockSpec((1, d, e), lambda i, j: (layer, 0, 0)),
                  pl.BlockSpec((1, 1, e), lambda i, j: (layer, 0, 0))],
        out_specs=[packed, top, top],
        out_shape=[_sds((b, l, d // 2), U32), _sds((b, l, TOP_K), I32), _sds((b, l, TOP_K), F32)],
        compiler_params=_params(2), name="router",
    )(x, sc, sh, router_w, router_b.reshape(router_b.shape[0], 1, e))


def _moe_plan(idx_flat, tm, n_tiles):
    e = N_EXPERTS
    oh = (idx_flat[:, None] == jnp.arange(e, dtype=I32)[None, :]).astype(I32)
    csum = jnp.cumsum(oh, axis=0)
    counts = csum[-1]
    tiles_e = (counts + tm - 1) // tm
    tile_end = jnp.cumsum(tiles_e)
    row_start = (tile_end - tiles_e) * tm
    pos = jnp.sum(oh * (row_start[None, :] + csum - oh), axis=1).astype(I32)
    n_valid = tile_end[-1]
    tile_id = jnp.arange(n_tiles, dtype=I32)
    tile_row = jnp.minimum(tile_id, n_valid - 1)
    expert_of = lambda t: jnp.sum((t[:, None] >= tile_end[None, :]).astype(I32), axis=1).astype(I32)
    tile_e = expert_of(tile_row)
    prev_e = jnp.concatenate([jnp.full((1,), -1, I32), tile_e[:-1]])
    first = ((tile_id < n_valid) & (tile_e != prev_e)).astype(I32)
    run_end = tile_end[tile_e]
    nxt = jnp.where(run_end < n_valid, expert_of(jnp.minimum(run_end, n_valid - 1)), -1).astype(I32)
    grp = (jnp.cumsum(first) - 1).astype(I32)
    counts2 = jnp.stack([n_valid, jnp.sum(first)]).astype(I32)
    rows = jnp.clip(counts[tile_e] - (tile_row - (tile_end - tiles_e)[tile_e]) * tm, 0, tm).astype(I32)
    return pos, (tile_row, tile_e, first, nxt, grp, counts2, rows)


def _dispatch_kernel(pos_ref, h_ref, xs_in_ref, xs_ref, sem, *, tt, base):
    del xs_in_ref
    t0 = pl.program_id(0) * tt

    def row_copy(t, r):
        return pltpu.make_async_copy(h_ref.at[pl.ds(t, 1), :], xs_ref.at[pl.ds(r, 1), :], sem)

    def issue(t, c):
        for k in range(TOP_K):
            row_copy(t, pos_ref[base + (t0 + t) * TOP_K + k]).start()
        return c

    lax.fori_loop(0, tt, issue, 0)

    def drain(t, c):
        for k in range(TOP_K):
            row_copy(0, 0).wait()
        return c

    lax.fori_loop(0, tt, drain, 0)


def _dispatch(pos, h2d, xs, base):
    t, d = h2d.shape
    tt = _row_tile(t, 256)
    return pl.pallas_call(
        functools.partial(_dispatch_kernel, tt=tt, base=base),
        grid_spec=pltpu.PrefetchScalarGridSpec(
            num_scalar_prefetch=1, grid=(t // tt,),
            in_specs=[pl.BlockSpec((tt, d), lambda i, pos: (i, 0)), pl.BlockSpec(memory_space=pl.ANY)],
            out_specs=pl.BlockSpec(memory_space=pl.ANY),
            scratch_shapes=[pltpu.SemaphoreType.DMA]),
        out_shape=_sds(xs.shape, xs.dtype), input_output_aliases={2: 0},
        compiler_params=_params(1), name="moe_dispatch")(pos, h2d, xs)


def _expert_weights(w_hbm, wbuf, wbf_ref, sem, sched, layer, col_blocks, tn):
    tile_e, first, nxt, grp, counts = sched
    n, r = pl.program_id(0), pl.program_id(1)
    slot = (n * counts[1] + grp[r]) % 2

    def copies(e, nn, s):
        return [pltpu.make_async_copy(
            w_hbm.at[layer, e, :, pl.ds(pl.multiple_of((nn + cb) * tn, tn), tn)], wbuf.at[s, c], sem.at[s])
            for c, cb in enumerate(col_blocks)]

    @pl.when((r < counts[0]) & (first[r] == 1))
    def _():
        @pl.when((n == 0) & (r == 0))
        def _():
            for cp in copies(tile_e[0], 0, 0):
                cp.start()

        for cp in copies(tile_e[r], n, slot):
            cp.wait()
        for c in range(len(col_blocks)):
            wbf_ref[c] = wbuf[slot, c].astype(BF16)

        @pl.when(nxt[r] >= 0)
        def _():
            for cp in copies(nxt[r], n, 1 - slot):
                cp.start()

        @pl.when((nxt[r] < 0) & (n + 1 < pl.num_programs(0)))
        def _():
            for cp in copies(tile_e[0], n + 1, 1 - slot):
                cp.start()


def _for_occupied_rows(rows, tm, fn):
    for nr in range(MOE_ROW_STEP, tm + 1, MOE_ROW_STEP):
        @pl.when((rows > nr - MOE_ROW_STEP) & (rows <= nr))
        def _(nr=nr):
            fn(nr)


def _pack_bf16_pair(lo, hi):
    bits = lambda x: lax.bitcast_convert_type(x.astype(BF16).astype(F32), U32)
    return (bits(hi) & jnp.uint32(0xFFFF0000)) | (bits(lo) >> 16)


def _unpack_bf16_pair(xp):
    lo = lax.bitcast_convert_type(xp << 16, F32).astype(BF16)
    hi = lax.bitcast_convert_type(xp & jnp.uint32(0xFFFF0000), F32).astype(BF16)
    return lo, hi


def _moe_up_kernel(tr_ref, te_ref, first_ref, nxt_ref, grp_ref, cnt_ref, rows_ref, x_ref, bg_ref, bl_ref, w_hbm,
                   o_ref, wbuf, wbf_ref, sem, *, layer, tn, nf):
    _expert_weights(w_hbm, wbuf, wbf_ref, sem, (te_ref, first_ref, nxt_ref, grp_ref, cnt_ref), layer, (0, nf), tn)
    r = pl.program_id(1)
    half = x_ref.shape[1]

    def rows_block(nr):
        lo, hi = _unpack_bf16_pair(x_ref[0:nr, :])

        def proj(c):
            return (jnp.dot(lo, wbf_ref[c, 0:half, :], preferred_element_type=F32)
                    + jnp.dot(hi, wbf_ref[c, half:2 * half, :], preferred_element_type=F32))

        xg = jnp.minimum(proj(0) + bg_ref[0, 0], SWIGLU_LIMIT)
        xl = jnp.clip(proj(1) + bl_ref[0, 0], -SWIGLU_LIMIT, SWIGLU_LIMIT)
        o_ref[0:nr, :] = (xg * jax.nn.sigmoid(SWIGLU_ALPHA * xg) * (xl + 1.0)).astype(o_ref.dtype)

    @pl.when(r < cnt_ref[0])
    def _():
        _for_occupied_rows(rows_ref[r], x_ref.shape[0], rows_block)


def _moe_down_kernel(tr_ref, te_ref, first_ref, nxt_ref, grp_ref, cnt_ref, rows_ref, a_ref, b_ref, w_hbm,
                     o_ref, wbuf, wbf_ref, sem, *, layer, tn):
    _expert_weights(w_hbm, wbuf, wbf_ref, sem, (te_ref, first_ref, nxt_ref, grp_ref, cnt_ref), layer, (0,), tn)
    r = pl.program_id(1)

    def rows_block(nr):
        o_ref[0:nr, :] = jnp.dot(a_ref[0:nr, :], wbf_ref[0], preferred_element_type=F32) + b_ref[0, 0]

    @pl.when(r < cnt_ref[0])
    def _():
        _for_occupied_rows(rows_ref[r], a_ref.shape[0], rows_block)


def _moe_experts(xs, sched, w1, b1, w2, b2, layer):
    r_pad, d = xs.shape[0], 2 * xs.shape[1]
    tm, tn = MOE_TM, MOE_TN_UP
    n_tiles = r_pad // tm
    f = w2.shape[2]
    nf = f // tn
    depth, e = b1.shape[0], b1.shape[1]
    b1r = b1.reshape(depth, e, 1, 2 * f)
    b2r = b2.reshape(depth, e, 1, d)
    ns = len(sched)
    imap = lambda fn: (lambda n, r, tr, te, *_: fn(n, r, tr, te))
    act = pl.pallas_call(
        functools.partial(_moe_up_kernel, layer=layer, tn=tn, nf=nf),
        grid_spec=pltpu.PrefetchScalarGridSpec(
            num_scalar_prefetch=ns, grid=(nf, n_tiles),
            in_specs=[pl.BlockSpec((tm, d // 2), imap(lambda n, r, tr, te: (tr[r], 0))),
                      pl.BlockSpec((1, 1, 1, tn), imap(lambda n, r, tr, te: (layer, te[r], 0, n))),
                      pl.BlockSpec((1, 1, 1, tn), imap(lambda n, r, tr, te: (layer, te[r], 0, n + nf))),
                      pl.BlockSpec(memory_space=pl.ANY)],
            out_specs=pl.BlockSpec((tm, tn), imap(lambda n, r, tr, te: (tr[r], n))),
            scratch_shapes=[pltpu.VMEM((2, 2, d, tn), F32), pltpu.VMEM((2, d, tn), BF16),
                            pltpu.SemaphoreType.DMA((2,))]),
        out_shape=_sds((r_pad, f), BF16), compiler_params=_params(2), name="moe_up",
    )(*sched, xs, b1r, b1r, w1)
    tn = MOE_TN_DOWN
    nd = d // tn
    return pl.pallas_call(
        functools.partial(_moe_down_kernel, layer=layer, tn=tn),
        grid_spec=pltpu.PrefetchScalarGridSpec(
            num_scalar_prefetch=ns, grid=(nd, n_tiles),
            in_specs=[pl.BlockSpec((tm, f), imap(lambda n, r, tr, te: (tr[r], 0))),
                      pl.BlockSpec((1, 1, 1, tn), imap(lambda n, r, tr, te: (layer, te[r], 0, n))),
                      pl.BlockSpec(memory_space=pl.ANY)],
            out_specs=pl.BlockSpec((tm, tn), imap(lambda n, r, tr, te: (tr[r], n))),
            scratch_shapes=[pltpu.VMEM((2, 1, f, tn), F32), pltpu.VMEM((1, f, tn), BF16),
                            pltpu.SemaphoreType.DMA((2,))]),
        out_shape=_sds((r_pad, d), F32), compiler_params=_params(2), name="moe_down",
    )(*sched, act, b2r, w2)


def _combine_kernel(pos_ref, x_ref, g_ref, prob_ref, ys_ref, o_ref, buf_ref, sem, *, tt, base):
    t0 = (pl.program_id(0) * pl.num_programs(1) + pl.program_id(1)) * tt

    def row_copy(t, k, r):
        return pltpu.make_async_copy(ys_ref.at[pl.ds(r, 1), :], buf_ref.at[k, pl.ds(t, 1), :], sem)

    def issue(t, c):
        for k in range(TOP_K):
            row_copy(t, k, pos_ref[base + (t0 + t) * TOP_K + k]).start()
        return c

    lax.fori_loop(0, tt, issue, 0)

    def drain(t, c):
        for k in range(TOP_K):
            row_copy(0, 0, 0).wait()
        return c

    lax.fori_loop(0, tt, drain, 0)
    prob = prob_ref[0]
    acc = prob[:, 0:1] * buf_ref[0]
    for k in range(1, TOP_K):
        acc = acc + prob[:, k:k + 1] * buf_ref[k]
    o_ref[0] = x_ref[0] + g_ref[0] * acc


def _combine(pos, x, g, prob, ys, base):
    b, l, d = x.shape
    tt = _row_tile(l, 128)
    row = pl.BlockSpec((1, tt, d), lambda i, j, pos: (i, j, 0))
    return pl.pallas_call(
        functools.partial(_combine_kernel, tt=tt, base=base),
        grid_spec=pltpu.PrefetchScalarGridSpec(
            num_scalar_prefetch=1, grid=(b, l // tt),
            in_specs=[row, pl.BlockSpec((1, 1, d), lambda i, j, pos: (i, 0, 0)),
                      pl.BlockSpec((1, tt, TOP_K), lambda i, j, pos: (i, j, 0)),
                      pl.BlockSpec(memory_space=pl.ANY)],
            out_specs=row,
            scratch_shapes=[pltpu.VMEM((TOP_K, tt, d), F32), pltpu.SemaphoreType.DMA]),
        out_shape=_sds((b, l, d), F32), compiler_params=_params(2), name="moe_combine")(pos, x, g, prob, ys)


def _moe_rows(n_tokens):
    return (-(-n_tokens * TOP_K // MOE_TM) + N_EXPERTS) * MOE_TM


def _moe_layer(xs_groups, mods, xs, router_w, router_b, w1, b1, w2, b2, layer):
    routed = [_router(x, sc, sh, router_w, router_b, layer) for x, (sc, sh, _) in zip(xs_groups, mods)]
    idx_flat = jnp.concatenate([r[1].reshape(-1) for r in routed])
    pos, sched = _moe_plan(idx_flat, MOE_TM, xs.shape[0] // MOE_TM)
    bases, base = [], 0
    for h, _, _ in routed:
        bases.append(base)
        xs = _dispatch(pos, h.reshape(-1, h.shape[-1]), xs, base)
        base += h.shape[0] * h.shape[1] * TOP_K
    ys = _moe_experts(xs, sched, w1, b1, w2, b2, layer)
    outs = [_combine(pos, x, g, r[2], ys, bs) for x, (_, _, g), r, bs in zip(xs_groups, mods, routed, bases)]
    return outs, xs


def kernel(x_prompt, x_sample, c_prompt, c_sample, cache_k, cache_v, state_conv, state_pool, page_table, ada_w, ada_b, w_in_even, w_in_odd, w_out, conv_w, conv_b, conv_ln_g, conv_ln_b, pool_w, pool_scale, router_w, router_b, moe_w1, moe_b1, moe_w2, moe_b2, final_g):
    depth = ada_w.shape[0]
    d = x_prompt.shape[-1]
    bp, lp, _ = x_prompt.shape
    bs, ls, _ = x_sample.shape
    past_len = page_table.shape[1] * cache_k.shape[2]
    pool_ch = pool_scale.shape[1]

    n_c = bp + bs
    c_all = jnp.pad(jnp.concatenate([c_prompt, c_sample], axis=0), ((0, -n_c % 8), (0, 0)))
    mod_all = _adaln(c_all, ada_w, ada_b)

    def mods(i, lo, hi):
        m = mod_all[i, lo:hi].reshape(hi - lo, 1, 6, d)
        return [m[:, :, n, :] for n in range(6)]

    xs = [x_prompt, x_sample]
    groups = [(0, bp), (bp, n_c)]
    conv_pre = [jnp.zeros((state_conv.shape[0], bp) + state_conv.shape[2:], F32), state_conv]
    pool_pre = [jnp.zeros((state_pool.shape[0], bp) + state_pool.shape[2:], F32), state_pool]
    starts = [0, past_len]
    new_k, new_v, new_conv, new_pool = [[], []], [[], []], [[], []], [[], []]
    moe_rows = jnp.zeros((_moe_rows(bp * lp + bs * ls), d // 2), U32)

    for i in range(depth):
        j = i // 2
        mod = [mods(i, lo, hi) for lo, hi in groups]
        for gi in range(2):
            x = xs[gi]
            b, l, _ = x.shape
            sh1, sc1, g1 = mod[gi][0], mod[gi][1], mod[gi][2]
            h = _norm_mod(x, sc1, sh1)
            as_rows = (lambda a: a.reshape(1, b * l, a.shape[-1])) if gi == 1 else (lambda a: a)
            if i % 2 == 0:
                p = _mm(as_rows(h), w_in_even, j).reshape(b, l, -1)
                qkv0, tok0 = 0, 3 * ATTN_WIDTH
            else:
                p = _mm(as_rows(h), w_in_odd, j).reshape(b, l, -1)
                qkv0, tok0 = pool_ch, 0
            q, k, v = [p[..., qkv0 + n * ATTN_WIDTH:qkv0 + (n + 1) * ATTN_WIDTH] for n in range(3)]
            new_k[gi].append(k.reshape(b, l, N_HEADS, HEAD_DIM))
            new_v[gi].append(v.reshape(b, l, N_HEADS, HEAD_DIM))
            if gi == 0:
                o_att = _moba_prompt(p) if i % 2 == 0 else _sb_prompt(p, qkv0)
            else:
                attend = _moba_sample if i % 2 == 0 else _sb_sample
                o_att = attend(q, k, v, cache_k, cache_v, page_table, i).astype(BF16)
            if i % 2 == 0:
                o_tok, u = _conv_module(p, tok0, conv_pre[gi][j], conv_w[j], conv_b[j], conv_ln_g[j], conv_ln_b[j])
                full = jnp.concatenate([conv_pre[gi][j], u[:, -min(l, CONV_WIDTH - 1):]], axis=1)
                new_conv[gi].append(full[:, -(CONV_WIDTH - 1):])
            else:
                o_tok = _pool_module(p, pool_pre[gi][j], starts[gi], pool_w, j, pool_scale[j])
                n_keep = pool_pre[gi].shape[2]
                if gi == 0:
                    new_pool[gi].append(p[:, -n_keep:, :pool_ch])
                else:
                    full = jnp.concatenate([pool_pre[gi][j], p[..., :pool_ch]], axis=1)
                    new_pool[gi].append(full[:, -n_keep:])
            g_rows = jnp.broadcast_to(g1, x.shape) if gi == 1 else g1
            xs[gi] = _mm_out(as_rows(o_att), as_rows(o_tok), w_out, i, as_rows(x), as_rows(g_rows)).reshape(x.shape)
        moe_mods = [(m[4], m[3], m[5]) for m in mod]
        xs, moe_rows = _moe_layer(xs, moe_mods, moe_rows, router_w, router_b, moe_w1, moe_b1, moe_w2, moe_b2, i)

    y = [_final_norm(x, final_g) for x in xs]
    stack = lambda parts: jnp.stack(parts)
    return (y[0], y[1], stack(new_k[0]), stack(new_v[0]), stack(new_conv[0]), stack(new_pool[0]),
            stack(new_k[1]), stack(new_v[1]), stack(new_conv[1]), stack(new_pool[1]))
```

```python
import functools
import math

import jax
import jax.numpy as jnp
from jax import lax
from jax.experimental import pallas as pl
from jax.experimental.pallas import tpu as pltpu

F32 = jnp.float32
BF16 = jnp.bfloat16
I32 = jnp.int32
U32 = jnp.uint32

N_HEADS = 8
HEAD_DIM = 128
ATTN_WIDTH = N_HEADS * HEAD_DIM
MOBA_BLOCK = 256
MOBA_TOPK = 3
CONV_WIDTH = 31
CONV_HALO = 32
POOL_WINDOWS = (2, 4, 8, 16)
POOL_GROUP = 256
POOL_HALO = 16
N_EXPERTS = 32
TOP_K = 4
SWIGLU_LIMIT = 7.0
SWIGLU_ALPHA = 1.702
EPS = 1e-6
ATTN_SCALE = 1.0 / math.sqrt(HEAD_DIM)

VMEM_LIMIT_V7X = 56 * 1024 * 1024
MOE_TM = 512
MOE_ROW_STEP = 128
MOE_TN_UP = 512
MOE_TN_DOWN = 1024
HEADS_PER_STEP = 4
QPAD = 8
PAGES_PER_STEP = 4
NT_DIMS = (((1,), (1,)), ((), ()))


def _params(n_axes):
    return pltpu.CompilerParams(dimension_semantics=("arbitrary",) * n_axes,
                                vmem_limit_bytes=VMEM_LIMIT_V7X)


def _sds(shape, dtype):
    return jax.ShapeDtypeStruct(shape, dtype)


def _adaln_kernel(c_ref, w_ref, b_ref, o_ref):
    c = c_ref[...]
    a = (c * jax.nn.sigmoid(c)).astype(BF16)
    o_ref[0] = jnp.dot(a, w_ref[0].astype(BF16), preferred_element_type=F32) + b_ref[0]


def _adaln(c_all, ada_w, ada_b):
    depth, d, n = ada_w.shape
    r = c_all.shape[0]
    tn = 1024
    return pl.pallas_call(
        _adaln_kernel, grid=(depth, n // tn),
        in_specs=[pl.BlockSpec((r, d), lambda i, j: (0, 0)),
                  pl.BlockSpec((1, d, tn), lambda i, j: (i, 0, j)),
                  pl.BlockSpec((1, 1, tn), lambda i, j: (i, 0, j))],
        out_specs=pl.BlockSpec((1, r, tn), lambda i, j: (i, 0, j)),
        out_shape=_sds((depth, r, n), F32), compiler_params=_params(2), name="adaln",
    )(c_all, ada_w, ada_b.reshape(depth, 1, n))


def _rms(x):
    return x * lax.rsqrt(jnp.mean(x * x, axis=-1, keepdims=True) + EPS)


def _norm_mod_kernel(x_ref, sc_ref, sh_ref, o_ref):
    o_ref[0] = (_rms(x_ref[0]) * (1.0 + sc_ref[0]) + sh_ref[0]).astype(o_ref.dtype)


def _row_tile(l, cap):
    return l if l <= cap else cap


def _col_tile(l):
    return 512 if l >= 512 else 1024


def _norm_mod(x, sc, sh):
    b, l, d = x.shape
    tl = _row_tile(l, 512)
    row = pl.BlockSpec((1, tl, d), lambda i, j: (i, j, 0))
    per_b = pl.BlockSpec((1, 1, d), lambda i, j: (i, 0, 0))
    return pl.pallas_call(_norm_mod_kernel, grid=(b, l // tl), in_specs=[row, per_b, per_b], out_specs=row,
                          out_shape=_sds((b, l, d), BF16), compiler_params=_params(2), name="norm_mod")(x, sc, sh)


def _final_norm_kernel(x_ref, g_ref, o_ref):
    o_ref[0] = _rms(x_ref[0]) * g_ref[...]


def _final_norm(x, g):
    b, l, d = x.shape
    tl = _row_tile(l, 512)
    row = pl.BlockSpec((1, tl, d), lambda i, j: (i, j, 0))
    return pl.pallas_call(_final_norm_kernel, grid=(b, l // tl),
                          in_specs=[row, pl.BlockSpec((1, d), lambda i, j: (0, 0))], out_specs=row,
                          out_shape=_sds((b, l, d), F32), compiler_params=_params(2), name="final_norm")(x, g.reshape(1, d))


def _first_row_tile():
    return (pl.program_id(1) == 0) & (pl.program_id(2) == 0)


def _mm_kernel(a_ref, w_ref, o_ref, wbf_ref):
    @pl.when(_first_row_tile())
    def _():
        wbf_ref[...] = w_ref[0].astype(BF16)

    o_ref[0] = jnp.dot(a_ref[0], wbf_ref[...], preferred_element_type=F32)


def _mm(a, w, layer):
    b, l, k = a.shape
    n = w.shape[2]
    tl, tn = _row_tile(l, 512), _col_tile(l)
    return pl.pallas_call(
        _mm_kernel, grid=(n // tn, b, l // tl),
        in_specs=[pl.BlockSpec((1, tl, k), lambda j, bi, li: (bi, li, 0)),
                  pl.BlockSpec((1, k, tn), lambda j, bi, li: (layer, 0, j))],
        out_specs=pl.BlockSpec((1, tl, tn), lambda j, bi, li: (bi, li, j)),
        out_shape=_sds((b, l, n), F32), scratch_shapes=[pltpu.VMEM((k, tn), BF16)],
        compiler_params=_params(3), name="in_proj")(a, w)


def _mm_out_kernel(a1_ref, a2_ref, w1_ref, w2_ref, x_ref, g_ref, o_ref, w1bf_ref, w2bf_ref):
    @pl.when(_first_row_tile())
    def _():
        w1bf_ref[...] = w1_ref[0].astype(BF16)
        w2bf_ref[...] = w2_ref[0].astype(BF16)

    y = jnp.dot(a1_ref[0], w1bf_ref[...], preferred_element_type=F32)
    y = y + jnp.dot(a2_ref[0], w2bf_ref[...], preferred_element_type=F32)
    o_ref[0] = x_ref[0] + g_ref[0] * y


def _mm_out(a1, a2, w, layer, x, g):
    b, l, kh = a1.shape
    n = w.shape[2]
    tl, tn = _row_tile(l, 512), _col_tile(l)
    g_spec = (pl.BlockSpec((1, 1, tn), lambda j, bi, li: (bi, 0, j)) if g.shape[1] == 1
              else pl.BlockSpec((1, tl, tn), lambda j, bi, li: (bi, li, j)))
    a_spec = pl.BlockSpec((1, tl, kh), lambda j, bi, li: (bi, li, 0))
    return pl.pallas_call(
        _mm_out_kernel, grid=(n // tn, b, l // tl),
        in_specs=[a_spec, a_spec,
                  pl.BlockSpec((1, kh, tn), lambda j, bi, li: (layer, 0, j)),
                  pl.BlockSpec((1, kh, tn), lambda j, bi, li: (layer, 1, j)),
                  pl.BlockSpec((1, tl, tn), lambda j, bi, li: (bi, li, j)),
                  g_spec],
        out_specs=pl.BlockSpec((1, tl, tn), lambda j, bi, li: (bi, li, j)),
        out_shape=_sds((b, l, n), F32),
        scratch_shapes=[pltpu.VMEM((kh, tn), BF16), pltpu.VMEM((kh, tn), BF16)],
        compiler_params=_params(3), name="out_proj")(a1, a2, w, w, x, g)


def _block_rank(s, idx, n, axis):
    rank = jnp.zeros(s.shape, I32)
    for j2 in range(n):
        c = s[:, j2:j2 + 1] if axis == 1 else s[j2:j2 + 1, :]
        rank = rank + ((c > s) | ((c == s) & (j2 < idx))).astype(I32)
    return rank


def _head_cols(h):
    return slice(h * HEAD_DIM, (h + 1) * HEAD_DIM)


def _kmean_kernel(k_ref, o_ref):
    n = pl.program_id(1)
    o_ref[0, pl.ds(n, 1), :] = jnp.mean(k_ref[0], axis=0, keepdims=True)


def _moba_prompt_kernel(q_ref, k_ref, v_ref, km_ref, sl_ref, o_ref):
    i = pl.program_id(2)
    t = MOBA_BLOCK
    nb = km_ref.shape[1]
    nh = q_ref.shape[2] // HEAD_DIM
    rowi = lax.broadcasted_iota(I32, (t, t), 0)
    coli = lax.broadcasted_iota(I32, (t, t), 1)
    eye = (rowi == coli).astype(BF16)
    blk_t = lax.broadcasted_iota(I32, (nb, t), 0)
    blk = lax.broadcasted_iota(I32, (t, nb), 1)
    qbs, sels, slopes = [], [], []
    for h in range(nh):
        q = q_ref[0, :, _head_cols(h)]
        s_t = lax.dot_general(km_ref[0, :, _head_cols(h)], q, NT_DIMS,
                              precision=lax.Precision.HIGHEST, preferred_element_type=F32)
        s_t = jnp.where(blk_t < i, s_t, -jnp.inf)
        sel_t = ((blk_t < i) & (_block_rank(s_t, blk_t, nb, 0) < MOBA_TOPK)).astype(BF16)
        sels.append(lax.dot_general(eye, sel_t, NT_DIMS, preferred_element_type=F32))
        qbs.append(q.astype(BF16))
        slopes.append(sl_ref[h][:, :1])

    def attend(h, j, bias, carry):
        m, l, acc = carry
        rows = pl.ds(pl.multiple_of(j * t, t), t)
        kj = k_ref[0, rows, _head_cols(h)].astype(BF16)
        vj = v_ref[0, rows, _head_cols(h)].astype(BF16)
        lg = lax.dot_general(qbs[h], kj, NT_DIMS, preferred_element_type=F32) * ATTN_SCALE
        lg = lg - slopes[h] * ((i - j) * t + rowi - coli).astype(F32) + bias
        m_new = jnp.maximum(m, jnp.max(lg, axis=1, keepdims=True))
        alpha = jnp.exp(m - m_new)
        p = jnp.exp(lg - m_new)
        l = alpha * l + jnp.sum(p, axis=1, keepdims=True)
        acc = alpha * acc + jnp.dot(p.astype(BF16), vj, preferred_element_type=F32)
        return m_new, l, acc

    init = (jnp.full((t, 1), -jnp.inf, F32), jnp.zeros((t, 1), F32), jnp.zeros((t, HEAD_DIM), F32))
    causal = jnp.where(coli <= rowi, 0.0, -jnp.inf)
    carry = tuple(attend(h, i, causal, init) for h in range(nh))

    def body(j, carry):
        out = []
        for h in range(nh):
            picked = jnp.max(jnp.where(blk == j, sels[h], 0.0), axis=1, keepdims=True)
            out.append(attend(h, j, jnp.where(picked > 0.5, 0.0, -jnp.inf), carry[h]))
        return tuple(out)

    carry = lax.fori_loop(0, i, body, carry)
    for h in range(nh):
        _, l, acc = carry[h]
        o_ref[0, :, _head_cols(h)] = (acc / l).astype(o_ref.dtype)


def _alibi_slopes():
    return jnp.asarray([2.0 ** (-8.0 * (h + 1) / N_HEADS) for h in range(N_HEADS)], F32)


def _moba_prompt(p):
    b, l, _ = p.shape
    t = MOBA_BLOCK
    nb = l // t
    kmean = pl.pallas_call(
        _kmean_kernel, grid=(b, nb),
        in_specs=[pl.BlockSpec((1, t, ATTN_WIDTH), lambda bi, n: (bi, n, 1))],
        out_specs=pl.BlockSpec((1, nb, ATTN_WIDTH), lambda bi, n: (bi, 0, 0)),
        out_shape=_sds((b, nb, ATTN_WIDTH), F32), compiler_params=_params(2), name="moba_kmean")(p)
    nh = HEADS_PER_STEP
    w = nh * HEAD_DIM
    hg = N_HEADS // nh
    slopes = jnp.broadcast_to(_alibi_slopes()[:, None, None], (N_HEADS, 1, HEAD_DIM))
    return pl.pallas_call(
        _moba_prompt_kernel, grid=(b, hg, nb),
        in_specs=[pl.BlockSpec((1, t, w), lambda bi, hi, i: (bi, i, hi)),
                  pl.BlockSpec((1, l, w), lambda bi, hi, i: (bi, 0, hg + hi)),
                  pl.BlockSpec((1, l, w), lambda bi, hi, i: (bi, 0, 2 * hg + hi)),
                  pl.BlockSpec((1, nb, w), lambda bi, hi, i: (bi, 0, hi)),
                  pl.BlockSpec((nh, 1, HEAD_DIM), lambda bi, hi, i: (hi, 0, 0))],
        out_specs=pl.BlockSpec((1, t, w), lambda bi, hi, i: (bi, i, hi)),
        out_shape=_sds((b, l, ATTN_WIDTH), BF16), compiler_params=_params(3), name="moba_prompt",
    )(p, p, p, kmean, slopes)


def _log_sigmoid_pair(z):
    t = jnp.log(1.0 + jnp.exp(-jnp.abs(z)))
    return jnp.minimum(z, 0.0) - t, jnp.minimum(-z, 0.0) - t


def _split_bf16(x):
    hi = x.astype(BF16)
    return hi, (x - hi.astype(F32)).astype(BF16)


def _suffix_sums(lk, later):
    hi, lo = _split_bf16(lk)
    return jnp.dot(hi, later, preferred_element_type=F32) + jnp.dot(lo, later, preferred_element_type=F32)


def _sb_prompt_kernel(q_ref, k_ref, v_ref, o_ref):
    i = pl.program_id(2)
    t = q_ref.shape[1]
    nh = q_ref.shape[2] // HEAD_DIM
    qbs = [q_ref[0, :, _head_cols(h)].astype(BF16) for h in range(nh)]
    rowi = lax.broadcasted_iota(I32, (t, t), 0)
    coli = lax.broadcasted_iota(I32, (t, t), 1)
    later = (rowi > coli).astype(BF16)

    def block(j, mask, carry):
        rows = pl.ds(pl.multiple_of(j * t, t), t)
        out = []
        for h in range(nh):
            c, acc = carry[h]
            kj = k_ref[0, rows, _head_cols(h)].astype(BF16)
            vj = v_ref[0, rows, _head_cols(h)].astype(BF16)
            z = lax.dot_general(qbs[h], kj, NT_DIMS, preferred_element_type=F32) * ATTN_SCALE
            ls_pos, ls_neg = _log_sigmoid_pair(z)
            lk = ls_neg if mask is None else jnp.where(mask, ls_neg, 0.0)
            a = jnp.exp(ls_pos + _suffix_sums(lk, later) + c)
            if mask is not None:
                a = jnp.where(mask, a, 0.0)
            acc = acc + jnp.dot(a.astype(BF16), vj, preferred_element_type=F32)
            out.append((c + jnp.sum(lk, axis=1, keepdims=True), acc))
        return tuple(out)

    init = tuple((jnp.zeros((t, 1), F32), jnp.zeros((t, HEAD_DIM), F32)) for _ in range(nh))
    carry = block(i, coli < rowi, init)
    carry = lax.fori_loop(1, i + 1, lambda step, carry: block(i - step, None, carry), carry)
    for h in range(nh):
        o_ref[0, :, _head_cols(h)] = carry[h][1].astype(o_ref.dtype)


def _sb_prompt(p, col0):
    b, l, _ = p.shape
    t = 256
    nh = HEADS_PER_STEP
    w = nh * HEAD_DIM
    hg = N_HEADS // nh
    c0 = col0 // w
    return pl.pallas_call(
        _sb_prompt_kernel, grid=(b, hg, l // t),
        in_specs=[pl.BlockSpec((1, t, w), lambda bi, hi, i: (bi, i, c0 + hi)),
                  pl.BlockSpec((1, l, w), lambda bi, hi, i: (bi, 0, c0 + hg + hi)),
                  pl.BlockSpec((1, l, w), lambda bi, hi, i: (bi, 0, c0 + 2 * hg + hi))],
        out_specs=pl.BlockSpec((1, t, w), lambda bi, hi, i: (bi, i, hi)),
        out_shape=_sds((b, l, ATTN_WIDTH), BF16), compiler_params=_params(3), name="sb_prompt")(p, p, p)


def _head_rows(ref, h):
    return ref[0, 0, pl.ds(h, ref.shape[2] // N_HEADS, stride=N_HEADS), :]


def _q_rows(q_ref, h):
    return q_ref[0, h * QPAD:(h + 1) * QPAD, :]


def _qk_scores(q_ref, k_ref, with_key_sums=False):
    zs, sums = [], []
    for h in range(N_HEADS):
        kh = _head_rows(k_ref, h)
        zs.append(lax.dot_general(_q_rows(q_ref, h).astype(BF16), kh.astype(BF16), NT_DIMS,
                                  preferred_element_type=F32))
        if with_key_sums:
            sums.append(jnp.sum(kh, axis=0, keepdims=True))
    z = jnp.concatenate(zs, axis=0) * ATTN_SCALE
    return (z, sums) if with_key_sums else z


def _pv(a, v_ref):
    outs = []
    for h in range(N_HEADS):
        ah = a[h * QPAD:(h + 1) * QPAD, :].astype(BF16)
        outs.append(jnp.dot(ah, _head_rows(v_ref, h).astype(BF16), preferred_element_type=F32))
    return jnp.concatenate(outs, axis=0)


def _query_index(shape):
    return lax.broadcasted_iota(I32, shape, 0) & (QPAD - 1)


def _sb_page(q_ref, k_ref, v_ref, mask, later, c):
    z = _qk_scores(q_ref, k_ref)
    ls_pos, ls_neg = _log_sigmoid_pair(z)
    lk = ls_neg if mask is None else jnp.where(mask, ls_neg, 0.0)
    a = jnp.exp(ls_pos + _suffix_sums(lk, later) + c)
    if mask is not None:
        a = jnp.where(mask, a, 0.0)
    return _pv(a, v_ref), c + jnp.sum(lk, axis=1, keepdims=True)


def _sb_sample_kernel(pt_ref, q_ref, kn_ref, vn_ref, *refs):
    del pt_ref
    pp = PAGES_PER_STEP
    k_refs, v_refs = refs[:pp], refs[pp:2 * pp]
    o_ref, c_ref = refs[2 * pp], refs[2 * pp + 1]
    rows = q_ref.shape[1]
    page = k_refs[0].shape[2] // N_HEADS
    later = (lax.broadcasted_iota(I32, (page, page), 0) > lax.broadcasted_iota(I32, (page, page), 1)).astype(BF16)

    @pl.when(pl.program_id(1) == 0)
    def _():
        mask = lax.broadcasted_iota(I32, (rows, page), 1) < _query_index((rows, page))
        o, c = _sb_page(q_ref, kn_ref, vn_ref, mask, later, jnp.zeros((rows, 1), F32))
        o_ref[0] = o
        c_ref[...] = c

    acc, c = o_ref[0], c_ref[...]
    for k_ref, v_ref in zip(k_refs, v_refs):
        o, c = _sb_page(q_ref, k_ref, v_ref, None, later, c)
        acc = acc + o
    o_ref[0] = acc
    c_ref[...] = c


def _stack_queries(q):
    b, lq, _ = q.shape
    assert lq <= QPAD
    q = q.reshape(b, lq, N_HEADS, HEAD_DIM).transpose(0, 2, 1, 3)
    return jnp.pad(q, ((0, 0), (0, 0), (0, QPAD - lq), (0, 0))).reshape(b, N_HEADS * QPAD, HEAD_DIM)


def _unstack_queries(o, lq):
    b = o.shape[0]
    o = o.reshape(b, N_HEADS, QPAD, HEAD_DIM)[:, :, :lq]
    return o.transpose(0, 2, 1, 3).reshape(b, lq, ATTN_WIDTH)


def _new_page(x, page):
    b, lq, _ = x.shape
    return jnp.pad(x, ((0, 0), (0, page - lq), (0, 0))).reshape(b, 1, page * N_HEADS, HEAD_DIM)


def _page_view(cache):
    depth, n_pool, page, h, d = cache.shape
    return cache.reshape(depth, n_pool, page * h, d)


def _sb_sample(q, k_new, v_new, cache_k, cache_v, page_table, layer):
    b, lq, _ = q.shape
    n_pages = page_table.shape[1]
    page = cache_k.shape[2]
    pp = PAGES_PER_STEP
    assert n_pages % pp == 0
    rows = N_HEADS * QPAD
    d = HEAD_DIM
    per_b = pl.BlockSpec((1, rows, d), lambda bi, s, pt: (bi, 0, 0))
    new = pl.BlockSpec((1, 1, page * N_HEADS, d), lambda bi, s, pt: (bi, 0, 0, 0))
    pg = lambda j: pl.BlockSpec((1, 1, page * N_HEADS, d),
                                lambda bi, s, pt: (layer, pt[bi, n_pages - 1 - (s * pp + j)], 0, 0))
    ck, cv = _page_view(cache_k), _page_view(cache_v)
    o = pl.pallas_call(
        _sb_sample_kernel,
        grid_spec=pltpu.PrefetchScalarGridSpec(
            num_scalar_prefetch=1, grid=(b, n_pages // pp),
            in_specs=[per_b, new, new] + [pg(j) for j in range(pp)] * 2,
            out_specs=per_b, scratch_shapes=[pltpu.VMEM((rows, 1), F32)]),
        out_shape=_sds((b, rows, d), F32), compiler_params=_params(2), name="sb_sample",
    )(page_table, _stack_queries(q), _new_page(k_new, page), _new_page(v_new, page), *([ck] * pp), *([cv] * pp))
    return _unstack_queries(o, lq)


def _moba_sample_part_kernel(pt_ref, q_ref, sl_ref, *refs, past_len):
    del pt_ref
    pp = PAGES_PER_STEP
    k_refs, v_refs = refs[:pp], refs[pp:2 * pp]
    o_ref, st_ref = refs[2 * pp], refs[2 * pp + 1]
    rows = q_ref.shape[1]
    page = k_refs[0].shape[2] // N_HEADS
    per_blk = MOBA_BLOCK // page
    qpos = past_len + _query_index((rows, page))
    lane = lax.broadcasted_iota(I32, (rows, page), 1)
    stat_lane = lax.broadcasted_iota(I32, (rows, HEAD_DIM), 1)
    for blk in range(pp // per_blk):
        n = pl.program_id(1) * (pp // per_blk) + blk
        lgs, key_sums = [], None
        for j in range(per_blk):
            z, sums = _qk_scores(q_ref, k_refs[blk * per_blk + j], with_key_sums=True)
            kpos = n * MOBA_BLOCK + j * page + lane
            lgs.append(z - sl_ref[...] * (qpos - kpos).astype(F32))
            key_sums = sums if key_sums is None else [a + b for a, b in zip(key_sums, sums)]
        m = functools.reduce(jnp.maximum, [jnp.max(lg, axis=1, keepdims=True) for lg in lgs])
        pes = [jnp.exp(lg - m) for lg in lgs]
        l = functools.reduce(jnp.add, [jnp.sum(pe, axis=1, keepdims=True) for pe in pes])
        o = functools.reduce(jnp.add, [_pv(pe, v_refs[blk * per_blk + j]) for j, pe in enumerate(pes)])
        s = jnp.concatenate([jnp.sum(_q_rows(q_ref, h) * (key_sums[h] * (1.0 / MOBA_BLOCK)), axis=1, keepdims=True)
                             for h in range(N_HEADS)], axis=0)
        o_ref[0, blk] = o
        st_ref[0, blk] = jnp.where(stat_lane == 0, m, jnp.where(stat_lane == 1, l, jnp.where(stat_lane == 2, s, 0.0)))


def _moba_sample_merge_kernel(q_ref, sl_ref, kn_ref, vn_ref, op_ref, m_ref, l_ref, s_ref, o_ref):
    rows = q_ref.shape[1]
    m, l, s = m_ref[0], l_ref[0], s_ref[0]
    nb = m.shape[1]
    sel = _block_rank(s, lax.broadcasted_iota(I32, (rows, nb), 1), nb, 1) < MOBA_TOPK
    z = _qk_scores(q_ref, kn_ref)
    dist = _query_index(z.shape) - lax.broadcasted_iota(I32, z.shape, 1)
    lg = jnp.where(dist >= 0, z - sl_ref[...] * dist.astype(F32), -jnp.inf)
    m_all = jnp.maximum(jnp.max(lg, axis=1, keepdims=True),
                        jnp.max(jnp.where(sel, m, -jnp.inf), axis=1, keepdims=True))
    pe = jnp.exp(lg - m_all)
    w = jnp.where(sel, jnp.exp(m - m_all), 0.0)
    denom = jnp.sum(pe, axis=1, keepdims=True) + jnp.sum(w * l, axis=1, keepdims=True)
    acc = _pv(pe, vn_ref)
    for n in range(nb):
        acc = acc + w[:, n:n + 1] * op_ref[0, n]
    o_ref[0] = acc / denom


def _moba_sample(q, k_new, v_new, cache_k, cache_v, page_table, layer):
    b, lq, _ = q.shape
    n_pages = page_table.shape[1]
    page = cache_k.shape[2]
    pp = PAGES_PER_STEP
    per_blk = MOBA_BLOCK // page
    assert MOBA_BLOCK % page == 0 and pp % per_blk == 0 and n_pages % pp == 0
    nb = n_pages // per_blk
    bps = pp // per_blk
    past_len = n_pages * page
    rows = N_HEADS * QPAD
    d = HEAD_DIM
    slopes = jnp.repeat(_alibi_slopes(), QPAD)[:, None]
    qs = _stack_queries(q)
    ck, cv = _page_view(cache_k), _page_view(cache_v)
    pg = lambda j: pl.BlockSpec((1, 1, page * N_HEADS, d), lambda bi, s, pt: (layer, pt[bi, s * pp + j], 0, 0))
    part = pl.BlockSpec((1, bps, rows, d), lambda bi, s, pt: (bi, s, 0, 0))
    o_part, stats = pl.pallas_call(
        functools.partial(_moba_sample_part_kernel, past_len=past_len),
        grid_spec=pltpu.PrefetchScalarGridSpec(
            num_scalar_prefetch=1, grid=(b, n_pages // pp),
            in_specs=[pl.BlockSpec((1, rows, d), lambda bi, s, pt: (bi, 0, 0)),
                      pl.BlockSpec((rows, 1), lambda bi, s, pt: (0, 0))] + [pg(j) for j in range(pp)] * 2,
            out_specs=[part, part]),
        out_shape=[_sds((b, nb, rows, d), F32)] * 2, compiler_params=_params(2), name="moba_sample_part",
    )(page_table, qs, slopes, *([ck] * pp), *([cv] * pp))
    m, l, s = [jnp.swapaxes(stats[..., c], 1, 2) for c in range(3)]
    per_b = pl.BlockSpec((1, rows, d), lambda bi: (bi, 0, 0))
    new = pl.BlockSpec((1, 1, page * N_HEADS, d), lambda bi: (bi, 0, 0, 0))
    stat = pl.BlockSpec((1, rows, nb), lambda bi: (bi, 0, 0))
    o = pl.pallas_call(
        _moba_sample_merge_kernel, grid=(b,),
        in_specs=[per_b, pl.BlockSpec((rows, 1), lambda bi: (0, 0)), new, new,
                  pl.BlockSpec((1, nb, rows, d), lambda bi: (bi, 0, 0, 0)), stat, stat, stat],
        out_specs=per_b, out_shape=_sds((b, rows, d), F32), compiler_params=_params(1), name="moba_sample_merge",
    )(qs, slopes, _new_page(k_new, page), _new_page(v_new, page), o_part, m, l, s)
    return _unstack_queries(o, lq)


def _conv_kernel(*refs, tl, use_halo):
    if use_halo:
        ga_ref, gb_ref, hga_ref, hgb_ref, pre_ref, w_ref, b_ref, g_ref, bt_ref, o_ref, u_ref, ext_ref, y_ref = refs
    else:
        ga_ref, gb_ref, pre_ref, w_ref, b_ref, g_ref, bt_ref, o_ref, u_ref, ext_ref, y_ref = refs
    i = pl.program_id(1)
    u = ga_ref[0] * jax.nn.sigmoid(gb_ref[0])
    u_ref[0] = u
    ext_ref[CONV_HALO:CONV_HALO + tl, :] = u

    @pl.when(i == 0)
    def _():
        ext_ref[0:CONV_HALO, :] = pre_ref[0]

    if use_halo:
        @pl.when(i > 0)
        def _():
            ext_ref[0:CONV_HALO, :] = hga_ref[0] * jax.nn.sigmoid(hgb_ref[0])

    lane = 128
    off = CONV_HALO - (CONV_WIDTH - 1)

    def chunk(c, carry):
        cs = pl.ds(pl.multiple_of(c * lane, lane), lane)
        acc = jnp.zeros((tl, lane), F32)
        for j in range(CONV_WIDTH):
            acc = acc + ext_ref[off + j:off + j + tl, cs] * w_ref[j:j + 1, cs]
        y_ref[:, cs] = acc
        return carry

    lax.fori_loop(0, ext_ref.shape[1] // lane, chunk, 0)
    y = y_ref[...] + b_ref[...]
    mu = jnp.mean(y, axis=-1, keepdims=True)
    var = jnp.mean(jnp.square(y - mu), axis=-1, keepdims=True)
    yn = (y - mu) * lax.rsqrt(var + EPS) * g_ref[...] + bt_ref[...]
    o_ref[0] = (yn * jax.nn.sigmoid(yn)).astype(o_ref.dtype)


def _conv_module(p, col0, prefix, w, bias, ln_g, ln_b):
    b, l, _ = p.shape
    c = w.shape[1]
    tl = _row_tile(l, 256)
    use_halo = l > tl
    ca, cb = col0 // c, col0 // c + 1
    pre = jnp.pad(prefix, ((0, 0), (CONV_HALO - prefix.shape[1], 0), (0, 0)))
    wp = jnp.pad(w, ((0, CONV_HALO - w.shape[0]), (0, 0)))
    row = lambda cc: pl.BlockSpec((1, tl, c), lambda bi, i: (bi, i, cc))
    hr = tl // CONV_HALO
    halo = lambda cc: pl.BlockSpec((1, CONV_HALO, c), lambda bi, i: (bi, jnp.maximum(i * hr - 1, 0), cc))
    vec = pl.BlockSpec((1, c), lambda bi, i: (0, 0))
    in_specs = [row(ca), row(cb)] + ([halo(ca), halo(cb)] if use_halo else []) + [
        pl.BlockSpec((1, CONV_HALO, c), lambda bi, i: (bi, 0, 0)),
        pl.BlockSpec((CONV_HALO, c), lambda bi, i: (0, 0)), vec, vec, vec]
    args = [p, p] + ([p, p] if use_halo else []) + [pre, wp, bias.reshape(1, c), ln_g.reshape(1, c), ln_b.reshape(1, c)]
    out_row = pl.BlockSpec((1, tl, c), lambda bi, i: (bi, i, 0))
    return pl.pallas_call(
        functools.partial(_conv_kernel, tl=tl, use_halo=use_halo), grid=(b, l // tl),
        in_specs=in_specs, out_specs=[out_row, out_row],
        out_shape=[_sds((b, l, c), BF16), _sds((b, l, c), F32)],
        scratch_shapes=[pltpu.VMEM((CONV_HALO + tl, c), F32), pltpu.VMEM((tl, c), F32)],
        compiler_params=_params(2), name="conv_module")(*args)


def _pool_kernel(*refs, tl, start, use_halo):
    if use_halo:
        u_ref, hu_ref, pre_ref, w_ref, sc_ref, o_ref, ext_ref = refs
    else:
        u_ref, pre_ref, w_ref, sc_ref, o_ref, ext_ref = refs
    i = pl.program_id(1)
    ext_ref[POOL_HALO:POOL_HALO + tl, :] = u_ref[0]

    @pl.when(i == 0)
    def _():
        ext_ref[0:POOL_HALO, :] = pre_ref[0]

    if use_halo:
        @pl.when(i > 0)
        def _():
            ext_ref[0:POOL_HALO, :] = hu_ref[0]

    pos1 = start + i * tl + lax.broadcasted_iota(I32, (tl, 1), 0) + 1
    for g, win in enumerate(POOL_WINDOWS):
        cs = slice(g * POOL_GROUP, (g + 1) * POOL_GROUP)
        ug = ext_ref[POOL_HALO:POOL_HALO + tl, cs]
        ws = ug
        for dlt in range(1, win):
            ws = ws + ext_ref[POOL_HALO - dlt:POOL_HALO - dlt + tl, cs]
        cnt = jnp.minimum(pos1, win).astype(F32)
        delta = ws / cnt - ug
        mixed = jnp.dot(delta.astype(BF16), w_ref[0, g].astype(BF16), preferred_element_type=F32)
        o_ref[0, :, cs] = (mixed * sc_ref[:, cs]).astype(o_ref.dtype)


def _pool_module(p, prefix, start, w_all, layer_j, scale):
    b, l, _ = p.shape
    c = scale.shape[0]
    tl = _row_tile(l, 256)
    use_halo = l > tl
    pre = jnp.pad(prefix, ((0, 0), (POOL_HALO - prefix.shape[1], 0), (0, 0)))
    row = pl.BlockSpec((1, tl, c), lambda bi, i: (bi, i, 0))
    hr = tl // POOL_HALO
    halo = pl.BlockSpec((1, POOL_HALO, c), lambda bi, i: (bi, jnp.maximum(i * hr - 1, 0), 0))
    in_specs = [row] + ([halo] if use_halo else []) + [
        pl.BlockSpec((1, POOL_HALO, c), lambda bi, i: (bi, 0, 0)),
        pl.BlockSpec((1,) + w_all.shape[1:], lambda bi, i: (layer_j, 0, 0, 0)),
        pl.BlockSpec((1, c), lambda bi, i: (0, 0))]
    args = [p] + ([p] if use_halo else []) + [pre, w_all, scale.reshape(1, c)]
    return pl.pallas_call(
        functools.partial(_pool_kernel, tl=tl, start=start, use_halo=use_halo), grid=(b, l // tl),
        in_specs=in_specs, out_specs=row, out_shape=_sds((b, l, c), BF16),
        scratch_shapes=[pltpu.VMEM((POOL_HALO + tl, c), F32)],
        compiler_params=_params(2), name="pool_module")(*args)


def _router_kernel(x_ref, sc_ref, sh_ref, rw_ref, rb_ref, h_ref, idx_ref, prob_ref):
    h = _rms(x_ref[0]) * (1.0 + sc_ref[0]) + sh_ref[0]
    half = h.shape[1] // 2
    h_ref[0] = _pack_bf16_pair(h[:, :half], h[:, half:])
    lg = jnp.dot(h, rw_ref[0], precision=lax.Precision.HIGHEST, preferred_element_type=F32) + rb_ref[0]
    lane = lax.broadcasted_iota(I32, lg.shape, 1)
    idxs, vals = [], []
    for _ in range(TOP_K):
        m = jnp.max(lg, axis=1, keepdims=True)
        ix = jnp.min(jnp.where(lg == m, lane, N_EXPERTS), axis=1, keepdims=True)
        idxs.append(ix)
        vals.append(m)
        lg = jnp.where(lane == ix, -jnp.inf, lg)
    es = [jnp.exp(v - vals[0]) for v in vals]
    tot = es[0] + es[1] + es[2] + es[3]
    idx_ref[0] = jnp.concatenate(idxs, axis=1)
    prob_ref[0] = jnp.concatenate([e / tot for e in es], axis=1)


def _router(x, sc, sh, router_w, router_b, layer):
    b, l, d = x.shape
    e = router_w.shape[2]
    tl = _row_tile(l, 256)
    row = pl.BlockSpec((1, tl, d), lambda i, j: (i, j, 0))
    packed = pl.BlockSpec((1, tl, d // 2), lambda i, j: (i, j, 0))
    per_b = pl.BlockSpec((1, 1, d), lambda i, j: (i, 0, 0))
    top = pl.BlockSpec((1, tl, TOP_K), lambda i, j: (i, j, 0))
    return pl.pallas_call(
        _router_kernel, grid=(b, l // tl),
        in_specs=[row, per_b, per_b, pl.BlockSpec((1, d, e), lambda i, j: (layer, 0, 0)),
                  pl.BlockSpec((1, 1, e), lambda i, j: (layer, 0, 0))],
        out_specs=[packed, top, top],
        out_shape=[_sds((b, l, d // 2), U32), _sds((b, l, TOP_K), I32), _sds((b, l, TOP_K), F32)],
        compiler_params=_params(2), name="router",
    )(x, sc, sh, router_w, router_b.reshape(router_b.shape[0], 1, e))


def _moe_plan(idx_flat, tm, n_tiles):
    e = N_EXPERTS
    oh = (idx_flat[:, None] == jnp.arange(e, dtype=I32)[None, :]).astype(I32)
    csum = jnp.cumsum(oh, axis=0)
    counts = csum[-1]
    tiles_e = (counts + tm - 1) // tm
    tile_end = jnp.cumsum(tiles_e)
    row_start = (tile_end - tiles_e) * tm
    pos = jnp.sum(oh * (row_start[None, :] + csum - oh), axis=1).astype(I32)
    n_valid = tile_end[-1]
    tile_id = jnp.arange(n_tiles, dtype=I32)
    tile_row = jnp.minimum(tile_id, n_valid - 1)
    expert_of = lambda t: jnp.sum((t[:, None] >= tile_end[None, :]).astype(I32), axis=1).astype(I32)
    tile_e = expert_of(tile_row)
    prev_e = jnp.concatenate([jnp.full((1,), -1, I32), tile_e[:-1]])
    first = ((tile_id < n_valid) & (tile_e != prev_e)).astype(I32)
    run_end = tile_end[tile_e]
    nxt = jnp.where(run_end < n_valid, expert_of(jnp.minimum(run_end, n_valid - 1)), -1).astype(I32)
    grp = (jnp.cumsum(first) - 1).astype(I32)
    counts2 = jnp.stack([n_valid, jnp.sum(first)]).astype(I32)
    rows = jnp.clip(counts[tile_e] - (tile_row - (tile_end - tiles_e)[tile_e]) * tm, 0, tm).astype(I32)
    return pos, (tile_row, tile_e, first, nxt, grp, counts2, rows)


def _dispatch_kernel(pos_ref, h_ref, xs_in_ref, xs_ref, sem, *, tt, base):
    del xs_in_ref
    t0 = pl.program_id(0) * tt

    def row_copy(t, r):
        return pltpu.make_async_copy(h_ref.at[pl.ds(t, 1), :], xs_ref.at[pl.ds(r, 1), :], sem)

    def issue(t, c):
        for k in range(TOP_K):
            row_copy(t, pos_ref[base + (t0 + t) * TOP_K + k]).start()
        return c

    lax.fori_loop(0, tt, issue, 0)

    def drain(t, c):
        for k in range(TOP_K):
            row_copy(0, 0).wait()
        return c

    lax.fori_loop(0, tt, drain, 0)


def _dispatch(pos, h2d, xs, base):
    t, d = h2d.shape
    tt = _row_tile(t, 256)
    return pl.pallas_call(
        functools.partial(_dispatch_kernel, tt=tt, base=base),
        grid_spec=pltpu.PrefetchScalarGridSpec(
            num_scalar_prefetch=1, grid=(t // tt,),
            in_specs=[pl.BlockSpec((tt, d), lambda i, pos: (i, 0)), pl.BlockSpec(memory_space=pl.ANY)],
            out_specs=pl.BlockSpec(memory_space=pl.ANY),
            scratch_shapes=[pltpu.SemaphoreType.DMA]),
        out_shape=_sds(xs.shape, xs.dtype), input_output_aliases={2: 0},
        compiler_params=_params(1), name="moe_dispatch")(pos, h2d, xs)


def _expert_weights(w_hbm, wbuf, wbf_ref, sem, sched, layer, col_blocks, tn):
    tile_e, first, nxt, grp, counts = sched
    n, r = pl.program_id(0), pl.program_id(1)
    slot = (n * counts[1] + grp[r]) % 2

    def copies(e, nn, s):
        return [pltpu.make_async_copy(
            w_hbm.at[layer, e, :, pl.ds(pl.multiple_of((nn + cb) * tn, tn), tn)], wbuf.at[s, c], sem.at[s])
            for c, cb in enumerate(col_blocks)]

    @pl.when((r < counts[0]) & (first[r] == 1))
    def _():
        @pl.when((n == 0) & (r == 0))
        def _():
            for cp in copies(tile_e[0], 0, 0):
                cp.start()

        for cp in copies(tile_e[r], n, slot):
            cp.wait()
        for c in range(len(col_blocks)):
            wbf_ref[c] = wbuf[slot, c].astype(BF16)

        @pl.when(nxt[r] >= 0)
        def _():
            for cp in copies(nxt[r], n, 1 - slot):
                cp.start()

        @pl.when((nxt[r] < 0) & (n + 1 < pl.num_programs(0)))
        def _():
            for cp in copies(tile_e[0], n + 1, 1 - slot):
                cp.start()


def _for_occupied_rows(rows, tm, fn):
    for nr in range(MOE_ROW_STEP, tm + 1, MOE_ROW_STEP):
        @pl.when((rows > nr - MOE_ROW_STEP) & (rows <= nr))
        def _(nr=nr):
            fn(nr)


def _pack_bf16_pair(lo, hi):
    bits = lambda x: lax.bitcast_convert_type(x.astype(BF16).astype(F32), U32)
    return (bits(hi) & jnp.uint32(0xFFFF0000)) | (bits(lo) >> 16)


def _unpack_bf16_pair(xp):
    lo = lax.bitcast_convert_type(xp << 16, F32).astype(BF16)
    hi = lax.bitcast_convert_type(xp & jnp.uint32(0xFFFF0000), F32).astype(BF16)
    return lo, hi


def _moe_up_kernel(tr_ref, te_ref, first_ref, nxt_ref, grp_ref, cnt_ref, rows_ref, x_ref, bg_ref, bl_ref, w_hbm,
                   o_ref, wbuf, wbf_ref, sem, *, layer, tn, nf):
    _expert_weights(w_hbm, wbuf, wbf_ref, sem, (te_ref, first_ref, nxt_ref, grp_ref, cnt_ref), layer, (0, nf), tn)
    r = pl.program_id(1)
    half = x_ref.shape[1]

    def rows_block(nr):
        lo, hi = _unpack_bf16_pair(x_ref[0:nr, :])

        def proj(c):
            return (jnp.dot(lo, wbf_ref[c, 0:half, :], preferred_element_type=F32)
                    + jnp.dot(hi, wbf_ref[c, half:2 * half, :], preferred_element_type=F32))

        xg = jnp.minimum(proj(0) + bg_ref[0, 0], SWIGLU_LIMIT)
        xl = jnp.clip(proj(1) + bl_ref[0, 0], -SWIGLU_LIMIT, SWIGLU_LIMIT)
        o_ref[0:nr, :] = (xg * jax.nn.sigmoid(SWIGLU_ALPHA * xg) * (xl + 1.0)).astype(o_ref.dtype)

    @pl.when(r < cnt_ref[0])
    def _():
        _for_occupied_rows(rows_ref[r], x_ref.shape[0], rows_block)


def _moe_down_kernel(tr_ref, te_ref, first_ref, nxt_ref, grp_ref, cnt_ref, rows_ref, a_ref, b_ref, w_hbm,
                     o_ref, wbuf, wbf_ref, sem, *, layer, tn):
    _expert_weights(w_hbm, wbuf, wbf_ref, sem, (te_ref, first_ref, nxt_ref, grp_ref, cnt_ref), layer, (0,), tn)
    r = pl.program_id(1)

    def rows_block(nr):
        o_ref[0:nr, :] = jnp.dot(a_ref[0:nr, :], wbf_ref[0], preferred_element_type=F32) + b_ref[0, 0]

    @pl.when(r < cnt_ref[0])
    def _():
        _for_occupied_rows(rows_ref[r], a_ref.shape[0], rows_block)


def _moe_experts(xs, sched, w1, b1, w2, b2, layer):
    r_pad, d = xs.shape[0], 2 * xs.shape[1]
    tm, tn = MOE_TM, MOE_TN_UP
    n_tiles = r_pad // tm
    f = w2.shape[2]
    nf = f // tn
    depth, e = b1.shape[0], b1.shape[1]
    b1r = b1.reshape(depth, e, 1, 2 * f)
    b2r = b2.reshape(depth, e, 1, d)
    ns = len(sched)
    imap = lambda fn: (lambda n, r, tr, te, *_: fn(n, r, tr, te))
    act = pl.pallas_call(
        functools.partial(_moe_up_kernel, layer=layer, tn=tn, nf=nf),
        grid_spec=pltpu.PrefetchScalarGridSpec(
            num_scalar_prefetch=ns, grid=(nf, n_tiles),
            in_specs=[pl.BlockSpec((tm, d // 2), imap(lambda n, r, tr, te: (tr[r], 0))),
                      pl.BlockSpec((1, 1, 1, tn), imap(lambda n, r, tr, te: (layer, te[r], 0, n))),
                      pl.BlockSpec((1, 1, 1, tn), imap(lambda n, r, tr, te: (layer, te[r], 0, n + nf))),
                      pl.BlockSpec(memory_space=pl.ANY)],
            out_specs=pl.BlockSpec((tm, tn), imap(lambda n, r, tr, te: (tr[r], n))),
            scratch_shapes=[pltpu.VMEM((2, 2, d, tn), F32), pltpu.VMEM((2, d, tn), BF16),
                            pltpu.SemaphoreType.DMA((2,))]),
        out_shape=_sds((r_pad, f), BF16), compiler_params=_params(2), name="moe_up",
    )(*sched, xs, b1r, b1r, w1)
    tn = MOE_TN_DOWN
    nd = d // tn
    return pl.pallas_call(
        functools.partial(_moe_down_kernel, layer=layer, tn=tn),
        grid_spec=pltpu.PrefetchScalarGridSpec(
            num_scalar_prefetch=ns, grid=(nd, n_tiles),
            in_specs=[pl.BlockSpec((tm, f), imap(lambda n, r, tr, te: (tr[r], 0))),
                      pl.BlockSpec((1, 1, 1, tn), imap(lambda n, r, tr, te: (layer, te[r], 0, n))),
                      pl.BlockSpec(memory_space=pl.ANY)],
            out_specs=pl.BlockSpec((tm, tn), imap(lambda n, r, tr, te: (tr[r], n))),
            scratch_shapes=[pltpu.VMEM((2, 1, f, tn), F32), pltpu.VMEM((1, f, tn), BF16),
                            pltpu.SemaphoreType.DMA((2,))]),
        out_shape=_sds((r_pad, d), F32), compiler_params=_params(2), name="moe_down",
    )(*sched, act, b2r, w2)


def _combine_kernel(pos_ref, x_ref, g_ref, prob_ref, ys_ref, o_ref, buf_ref, sem, *, tt, base):
    t0 = (pl.program_id(0) * pl.num_programs(1) + pl.program_id(1)) * tt

    def row_copy(t, k, r):
        return pltpu.make_async_copy(ys_ref.at[pl.ds(r, 1), :], buf_ref.at[k, pl.ds(t, 1), :], sem)

    def issue(t, c):
        for k in range(TOP_K):
            row_copy(t, k, pos_ref[base + (t0 + t) * TOP_K + k]).start()
        return c

    lax.fori_loop(0, tt, issue, 0)

    def drain(t, c):
        for k in range(TOP_K):
            row_copy(0, 0, 0).wait()
        return c

    lax.fori_loop(0, tt, drain, 0)
    prob = prob_ref[0]
    acc = prob[:, 0:1] * buf_ref[0]
    for k in range(1, TOP_K):
        acc = acc + prob[:, k:k + 1] * buf_ref[k]
    o_ref[0] = x_ref[0] + g_ref[0] * acc


def _combine(pos, x, g, prob, ys, base):
    b, l, d = x.shape
    tt = _row_tile(l, 256)
    row = pl.BlockSpec((1, tt, d), lambda i, j, pos: (i, j, 0))
    return pl.pallas_call(
        functools.partial(_combine_kernel, tt=tt, base=base),
        grid_spec=pltpu.PrefetchScalarGridSpec(
            num_scalar_prefetch=1, grid=(b, l // tt),
            in_specs=[row, pl.BlockSpec((1, 1, d), lambda i, j, pos: (i, 0, 0)),
                      pl.BlockSpec((1, tt, TOP_K), lambda i, j, pos: (i, j, 0)),
                      pl.BlockSpec(memory_space=pl.ANY)],
            out_specs=row,
            scratch_shapes=[pltpu.VMEM((TOP_K, tt, d), F32), pltpu.SemaphoreType.DMA]),
        out_shape=_sds((b, l, d), F32), compiler_params=_params(2), name="moe_combine")(pos, x, g, prob, ys)


def _moe_rows(n_tokens):
    return (-(-n_tokens * TOP_K // MOE_TM) + N_EXPERTS) * MOE_TM


def _moe_layer(xs_groups, mods, xs, router_w, router_b, w1, b1, w2, b2, layer):
    routed = [_router(x, sc, sh, router_w, router_b, layer) for x, (sc, sh, _) in zip(xs_groups, mods)]
    idx_flat = jnp.concatenate([r[1].reshape(-1) for r in routed])
    pos, sched = _moe_plan(idx_flat, MOE_TM, xs.shape[0] // MOE_TM)
    bases, base = [], 0
    for h, _, _ in routed:
        bases.append(base)
        xs = _dispatch(pos, h.reshape(-1, h.shape[-1]), xs, base)
        base += h.shape[0] * h.shape[1] * TOP_K
    ys = _moe_experts(xs, sched, w1, b1, w2, b2, layer)
    outs = [_combine(pos, x, g, r[2], ys, bs) for x, (_, _, g), r, bs in zip(xs_groups, mods, routed, bases)]
    return outs, xs


def kernel(x_prompt, x_sample, c_prompt, c_sample, cache_k, cache_v, state_conv, state_pool, page_table, ada_w, ada_b, w_in_even, w_in_odd, w_out, conv_w, conv_b, conv_ln_g, conv_ln_b, pool_w, pool_scale, router_w, router_b, moe_w1, moe_b1, moe_w2, moe_b2, final_g):
    depth = ada_w.shape[0]
    d = x_prompt.shape[-1]
    bp, lp, _ = x_prompt.shape
    bs, ls, _ = x_sample.shape
    past_len = page_table.shape[1] * cache_k.shape[2]
    pool_ch = pool_scale.shape[1]

    n_c = bp + bs
    c_all = jnp.pad(jnp.concatenate([c_prompt, c_sample], axis=0), ((0, -n_c % 8), (0, 0)))
    mod_all = _adaln(c_all, ada_w, ada_b)

    def mods(i, lo, hi):
        m = mod_all[i, lo:hi].reshape(hi - lo, 1, 6, d)
        return [m[:, :, n, :] for n in range(6)]

    xs = [x_prompt, x_sample]
    groups = [(0, bp), (bp, n_c)]
    conv_pre = [jnp.zeros((state_conv.shape[0], bp) + state_conv.shape[2:], F32), state_conv]
    pool_pre = [jnp.zeros((state_pool.shape[0], bp) + state_pool.shape[2:], F32), state_pool]
    starts = [0, past_len]
    new_k, new_v, new_conv, new_pool = [[], []], [[], []], [[], []], [[], []]
    moe_rows = jnp.zeros((_moe_rows(bp * lp + bs * ls), d // 2), U32)

    for i in range(depth):
        j = i // 2
        mod = [mods(i, lo, hi) for lo, hi in groups]
        for gi in range(2):
            x = xs[gi]
            b, l, _ = x.shape
            sh1, sc1, g1 = mod[gi][0], mod[gi][1], mod[gi][2]
            h = _norm_mod(x, sc1, sh1)
            as_rows = (lambda a: a.reshape(1, b * l, a.shape[-1])) if gi == 1 else (lambda a: a)
            if i % 2 == 0:
                p = _mm(as_rows(h), w_in_even, j).reshape(b, l, -1)
                qkv0, tok0 = 0, 3 * ATTN_WIDTH
            else:
                p = _mm(as_rows(h), w_in_odd, j).reshape(b, l, -1)
                qkv0, tok0 = pool_ch, 0
            q, k, v = [p[..., qkv0 + n * ATTN_WIDTH:qkv0 + (n + 1) * ATTN_WIDTH] for n in range(3)]
            new_k[gi].append(k.reshape(b, l, N_HEADS, HEAD_DIM))
            new_v[gi].append(v.reshape(b, l, N_HEADS, HEAD_DIM))
            if gi == 0:
                o_att = _moba_prompt(p) if i % 2 == 0 else _sb_prompt(p, qkv0)
            else:
                attend = _moba_sample if i % 2 == 0 else _sb_sample
                o_att = attend(q, k, v, cache_k, cache_v, page_table, i).astype(BF16)
            if i % 2 == 0:
                o_tok, u = _conv_module(p, tok0, conv_pre[gi][j], conv_w[j], conv_b[j], conv_ln_g[j], conv_ln_b[j])
                full = jnp.concatenate([conv_pre[gi][j], u[:, -min(l, CONV_WIDTH - 1):]], axis=1)
                new_conv[gi].append(full[:, -(CONV_WIDTH - 1):])
            else:
                o_tok = _pool_module(p, pool_pre[gi][j], starts[gi], pool_w, j, pool_scale[j])
                n_keep = pool_pre[gi].shape[2]
                if gi == 0:
                    new_pool[gi].append(p[:, -n_keep:, :pool_ch])
                else:
                    full = jnp.concatenate([pool_pre[gi][j], p[..., :pool_ch]], axis=1)
                    new_pool[gi].append(full[:, -n_keep:])
            g_rows = jnp.broadcast_to(g1, x.shape) if gi == 1 else g1
            xs[gi] = _mm_out(as_rows(o_att), as_rows(o_tok), w_out, i, as_rows(x), as_rows(g_rows)).reshape(x.shape)
        moe_mods = [(m[4], m[3], m[5]) for m in mod]
        xs, moe_rows = _moe_layer(xs, moe_mods, moe_rows, router_w, router_b, moe_w1, moe_b1, moe_w2, moe_b2, i)

    y = [_final_norm(x, final_g) for x in xs]
    stack = lambda parts: jnp.stack(parts)
    return (y[0], y[1], stack(new_k[0]), stack(new_v[0]), stack(new_conv[0]), stack(new_pool[0]),
            stack(new_k[1]), stack(new_v[1]), stack(new_conv[1]), stack(new_pool[1]))
```
